```python
import jax, jax.numpy as jnp
from jax import lax
import numpy as np


D_MODEL = 1024
BATCH = 16
SEQ = 2048
DEPTH = 4

RWKV_HEADS = 8
RWKV_HEAD_DIM = 64
RWKV_WIDTH = RWKV_HEADS * RWKV_HEAD_DIM
DECAY_LORA = 32
ICLR_LORA = 32
VRES_LORA = 32
GATE_LORA = 96
SGU_GROUPS = 4
SGU_WIDTH = 256
SGU_CHUNK = 128
CONV_WIDTH = 256
CONV_KERNEL = 31
N_BRANCHES = 3

OFF_R = 0
OFF_K = OFF_R + RWKV_WIDTH
OFF_V = OFF_K + RWKV_WIDTH
OFF_WL = OFF_V + RWKV_WIDTH
OFF_AL = OFF_WL + DECAY_LORA
OFF_GL = OFF_AL + ICLR_LORA
RWKV_COLS = OFF_GL + GATE_LORA
OFF_SGU = RWKV_COLS
OFF_CONV = OFF_SGU + 2 * SGU_WIDTH
OFF_GATE = OFF_CONV + 2 * CONV_WIDTH
IN_COLS = OFF_GATE + N_BRANCHES * D_MODEL
OFF_VRES = IN_COLS

N_EXPERTS = 32
N_GROUPS = 4
EXPERTS_PER_GROUP = N_EXPERTS // N_GROUPS
TOP_K = 2
D_EXPERT = 512
EXPERT_BLOCK = 256

ALPHA = (2 * DEPTH) ** 0.25
BETA = (8 * DEPTH) ** -0.25
LN_EPS = 1e-5
GN_EPS = 64e-5

kernel_name = 'hybrid_rwkv7_sgu_conformer_grouped_moe'


def layer_norm(x, g, b):
    xf = x.astype(jnp.float32)
    mu = jnp.mean(xf, axis=-1, keepdims=True)
    var = jnp.mean(jnp.square(xf - mu), axis=-1, keepdims=True)
    return ((xf - mu) * lax.rsqrt(var + LN_EPS)).astype(x.dtype) * g + b


def token_shift_mix(p, mu):
    prev = jnp.pad(p, ((0, 0), (1, 0), (0, 0)))[:, :-1]
    return p + (prev - p) * mu


def wkv7_scan(r, decay, k, v, a, b):
    bsz, _, h, n = r.shape

    def step(state, inp):
        r_t, w_t, k_t, v_t, a_t, b_t = inp
        sa = jnp.einsum('bhvk,bhk->bhv', state, a_t)
        state = (state * w_t[:, :, None, :] + sa[..., None] * b_t[:, :, None, :]
                 + v_t[..., None] * k_t[:, :, None, :])
        return state, jnp.einsum('bhvk,bhk->bhv', state, r_t)

    xs = tuple(jnp.moveaxis(t, 1, 0) for t in (r, decay, k, v, a, b))
    _, y = lax.scan(step, jnp.zeros((bsz, h, n, n), jnp.float32), xs)
    return jnp.moveaxis(y, 0, 1)


def rwkv7_branch(r, k, v, wl, al, gl, w0, w_up, a0, a_up, g_up, k_k, k_a, r_k, gn_g, gn_b, w_o):
    f32 = jnp.float32
    bsz, seq, _ = r.shape

    def heads(t):
        return t.astype(f32).reshape(bsz, seq, RWKV_HEADS, RWKV_HEAD_DIM)

    w_log = -jax.nn.softplus(-(w0 + jnp.tanh(wl) @ w_up)) - 0.5
    decay = jnp.exp(-jnp.exp(heads(w_log)))
    iclr = jax.nn.sigmoid(a0 + al @ a_up)
    gate = jax.nn.sigmoid(gl) @ g_up
    kk = heads(k * k_k)
    kk = kk / jnp.maximum(jnp.sqrt(jnp.sum(kk * kk, axis=-1, keepdims=True)), 1e-12)
    kh = heads(k * (1 + (iclr - 1) * k_a))
    rh, vh, ah = heads(r), heads(v), heads(iclr)
    y = wkv7_scan(rh, decay, kh, vh, -kk, kk * ah)
    mu = jnp.mean(y, axis=-1, keepdims=True)
    var = jnp.mean(jnp.square(y - mu), axis=-1, keepdims=True)
    hn = (RWKV_HEADS, RWKV_HEAD_DIM)
    yn = (y - mu) * lax.rsqrt(var + GN_EPS) * gn_g.astype(f32).reshape(hn) + gn_b.astype(f32).reshape(hn)
    bonus = jnp.sum(rh * kh * r_k.astype(f32), axis=-1, keepdims=True) * vh
    out = (yn + bonus).reshape(bsz, seq, RWKV_WIDTH).astype(r.dtype) * gate
    return out @ w_o


def sgu_branch(pb, ln_g, ln_b, w_s, b_s, w_o):
    z = jax.nn.gelu(pb, approximate=False)
    u, v = z[..., :SGU_WIDTH], z[..., SGU_WIDTH:]
    v = layer_norm(v, ln_g, ln_b)
    bsz, seq, _ = v.shape
    vg = v.reshape(bsz, seq // SGU_CHUNK, SGU_CHUNK, SGU_GROUPS, SGU_WIDTH // SGU_GROUPS)
    causal = jnp.tril(jnp.ones((SGU_CHUNK, SGU_CHUNK), dtype=bool))
    w_c = jnp.where(causal, w_s, 0)
    mixed = jnp.einsum('gpq,bcqgd->bcpgd', w_c, vg) + jnp.transpose(b_s)[:, :, None]
    return (u * mixed.reshape(bsz, seq, SGU_WIDTH)) @ w_o


def conv_branch(pc, dw_kernel, dw_bias, ln_g, ln_b, w_o):
    h = pc[..., :CONV_WIDTH] * jax.nn.sigmoid(pc[..., CONV_WIDTH:])
    h = lax.conv_general_dilated(
        h, dw_kernel[:, None, :], window_strides=(1,), padding=[(CONV_KERNEL - 1, 0)],
        dimension_numbers=('NWC', 'WIO', 'NWC'), feature_group_count=CONV_WIDTH) + dw_bias
    h = jax.nn.silu(layer_norm(h, ln_g, ln_b))
    return h @ w_o


def grouped_moe(x, router_w, router_b, w_gate, w_up, w_down):
    bsz, seq, d = x.shape
    t = bsz * seq
    xf = x.reshape(t, d)
    scores = jax.nn.sigmoid((xf @ router_w).astype(jnp.float32))
    sel = (scores + router_b.astype(jnp.float32)).reshape(t, N_GROUPS, EXPERTS_PER_GROUP)
    group_score = jnp.sum(lax.top_k(sel, TOP_K)[0], axis=-1)
    grp = jnp.argmax(group_score, axis=-1)
    sel_in = jnp.take_along_axis(sel, grp[:, None, None], axis=1)[:, 0]
    _, local = lax.top_k(sel_in, TOP_K)
    expert = grp[:, None] * EXPERTS_PER_GROUP + local
    gate = jnp.take_along_axis(scores, expert, axis=1)
    gate = gate / jnp.sum(gate, axis=-1, keepdims=True)
    tk = t * TOP_K
    flat_e = expert.reshape(tk)
    flat_tok = jnp.repeat(jnp.arange(t, dtype=jnp.int32), TOP_K)
    flat_g = gate.reshape(tk)
    order = jnp.argsort(flat_e)
    e_sorted = flat_e[order]
    counts = jnp.bincount(flat_e, length=N_EXPERTS)
    padded = (counts + EXPERT_BLOCK - 1) // EXPERT_BLOCK * EXPERT_BLOCK
    pad_end = jnp.cumsum(padded)
    pad_start = pad_end - padded
    start = jnp.cumsum(counts) - counts
    dest = pad_start[e_sorted] + jnp.arange(tk, dtype=jnp.int32) - start[e_sorted]
    n_blocks = (tk + EXPERT_BLOCK - 1) // EXPERT_BLOCK + N_EXPERTS
    rows = n_blocks * EXPERT_BLOCK
    buf_tok = jnp.full((rows,), t, jnp.int32).at[dest].set(flat_tok[order])
    buf_gate = jnp.zeros((rows,), jnp.float32).at[dest].set(flat_g[order])
    block_expert = jnp.clip(
        jnp.searchsorted(pad_end, jnp.arange(n_blocks, dtype=jnp.int32) * EXPERT_BLOCK, side='right'),
        0, N_EXPERTS - 1)
    xb = jnp.take(xf, buf_tok, axis=0, mode='fill', fill_value=0).reshape(n_blocks, EXPERT_BLOCK, d)

    def expert_block(args):
        xblk, e = args
        h = jax.nn.silu(xblk @ w_gate[e]) * (xblk @ w_up[e])
        return h @ w_down[e]

    yb = lax.map(expert_block, (xb, block_expert)).reshape(rows, d)
    y = jnp.zeros((t, d), yb.dtype).at[buf_tok].add(yb * buf_gate[:, None].astype(yb.dtype), mode='drop')
    return y.reshape(bsz, seq, d)


def setup_inputs(seed: int = 0) -> dict:
    key = jax.random.key(seed)
    ks = iter(jax.random.split(key, 48))

    def nrm(shape, scale):
        return jax.random.normal(next(ks), shape, jnp.float32) * scale

    def uni(shape, lo, hi):
        return jax.random.uniform(next(ks), shape, jnp.float32, lo, hi)

    d, L = D_MODEL, DEPTH
    return {
        'x': nrm((BATCH, SEQ, d), 1.0),
        'w_in_first': nrm((d, IN_COLS), d ** -0.5),
        'w_in_rest': nrm((L - 1, d, IN_COLS + VRES_LORA), d ** -0.5),
        'rwkv_mu': uni((L, RWKV_COLS), 0.0, 1.0),
        'rwkv_mu_vres': uni((L - 1, VRES_LORA), 0.0, 1.0),
        'rwkv_w0': uni((L, RWKV_WIDTH), -5.0, 0.0),
        'rwkv_w_up': nrm((L, DECAY_LORA, RWKV_WIDTH), 0.5 * DECAY_LORA ** -0.5),
        'rwkv_a0': nrm((L, RWKV_WIDTH), 0.5),
        'rwkv_a_up': nrm((L, ICLR_LORA, RWKV_WIDTH), 0.5 * ICLR_LORA ** -0.5),
        'rwkv_g_up': nrm((L, GATE_LORA, RWKV_WIDTH), GATE_LORA ** -0.5),
        'rwkv_v0': nrm((L - 1, RWKV_WIDTH), 0.5),
        'rwkv_v_up': nrm((L - 1, VRES_LORA, RWKV_WIDTH), 0.5 * VRES_LORA ** -0.5),
        'rwkv_k_k': 0.85 + nrm((L, RWKV_WIDTH), 0.1),
        'rwkv_k_a': 1.0 + nrm((L, RWKV_WIDTH), 0.1),
        'rwkv_r_k': nrm((L, RWKV_HEADS, RWKV_HEAD_DIM), 0.1),
        'rwkv_gn_g': 1.0 + nrm((L, RWKV_WIDTH), 0.1),
        'rwkv_gn_b': nrm((L, RWKV_WIDTH), 0.01),
        'rwkv_w_o': nrm((L, RWKV_WIDTH, d), BETA * RWKV_WIDTH ** -0.5),
        'sgu_ln_g': 1.0 + nrm((L, SGU_WIDTH), 0.1),
        'sgu_ln_b': nrm((L, SGU_WIDTH), 0.01),
        'sgu_w': nrm((L, SGU_GROUPS, SGU_CHUNK, SGU_CHUNK), SGU_CHUNK ** -0.5),
        'sgu_b': 1.0 + nrm((L, SGU_GROUPS, SGU_CHUNK), 0.1),
        'sgu_w_o': nrm((L, SGU_WIDTH, d), BETA * SGU_WIDTH ** -0.5),
        'conv_dw': nrm((L, CONV_KERNEL, CONV_WIDTH), CONV_KERNEL ** -0.5),
        'conv_db': nrm((L, CONV_WIDTH), 0.01),
        'conv_ln_g': 1.0 + nrm((L, CONV_WIDTH), 0.1),
        'conv_ln_b': nrm((L, CONV_WIDTH), 0.01),
        'conv_w_o': nrm((L, CONV_WIDTH, d), BETA * CONV_WIDTH ** -0.5),
        'w_out': nrm((L, d, d), BETA * d ** -0.5),
        'ln1_g': 1.0 + nrm((L, d), 0.1),
        'ln1_b': nrm((L, d), 0.01),
        'router_w': nrm((d, N_EXPERTS), d ** -0.5),
        'router_b': nrm((N_EXPERTS,), 0.01),
        'moe_w_gate': nrm((L, N_EXPERTS, d, D_EXPERT), d ** -0.5),
        'moe_w_up': nrm((L, N_EXPERTS, d, D_EXPERT), d ** -0.5),
        'moe_w_down': nrm((L, N_EXPERTS, D_EXPERT, d), BETA * D_EXPERT ** -0.5),
        'ln2_g': 1.0 + nrm((L, d), 0.1),
        'ln2_b': nrm((L, d), 0.01),
    }


def reference(x, w_in_first, w_in_rest, rwkv_mu, rwkv_mu_vres, rwkv_w0, rwkv_w_up, rwkv_a0, rwkv_a_up,
              rwkv_g_up, rwkv_v0, rwkv_v_up, rwkv_k_k, rwkv_k_a, rwkv_r_k, rwkv_gn_g, rwkv_gn_b, rwkv_w_o,
              sgu_ln_g, sgu_ln_b, sgu_w, sgu_b, sgu_w_o, conv_dw, conv_db, conv_ln_g, conv_ln_b, conv_w_o,
              w_out, ln1_g, ln1_b, router_w, router_b, moe_w_gate, moe_w_up, moe_w_down, ln2_g, ln2_b):
    v_first = None
    for l in range(DEPTH):
        w_in = w_in_first if l == 0 else w_in_rest[l - 1]
        proj = x @ w_in
        pa = token_shift_mix(proj[..., :RWKV_COLS], rwkv_mu[l])
        r = pa[..., OFF_R:OFF_K]
        k = pa[..., OFF_K:OFF_V]
        v = pa[..., OFF_V:OFF_WL]
        wl = pa[..., OFF_WL:OFF_AL]
        al = pa[..., OFF_AL:OFF_GL]
        gl = pa[..., OFF_GL:RWKV_COLS]
        if l == 0:
            v_first = v
        else:
            vl = token_shift_mix(proj[..., OFF_VRES:], rwkv_mu_vres[l - 1])
            v = v + (v_first - v) * jax.nn.sigmoid(rwkv_v0[l - 1] + vl @ rwkv_v_up[l - 1])
        o_a = rwkv7_branch(r, k, v, wl, al, gl, rwkv_w0[l], rwkv_w_up[l], rwkv_a0[l], rwkv_a_up[l],
                           rwkv_g_up[l], rwkv_k_k[l], rwkv_k_a[l], rwkv_r_k[l], rwkv_gn_g[l], rwkv_gn_b[l],
                           rwkv_w_o[l])
        o_b = sgu_branch(proj[..., OFF_SGU:OFF_CONV], sgu_ln_g[l], sgu_ln_b[l], sgu_w[l], sgu_b[l], sgu_w_o[l])
        o_c = conv_branch(proj[..., OFF_CONV:OFF_GATE], conv_dw[l], conv_db[l], conv_ln_g[l], conv_ln_b[l],
                          conv_w_o[l])
        gates = jax.nn.sigmoid(proj[..., OFF_GATE:IN_COLS])
        merged = (gates[..., :D_MODEL] * o_a + gates[..., D_MODEL:2 * D_MODEL] * o_b
                  + gates[..., 2 * D_MODEL:] * o_c)
        x = layer_norm(ALPHA * x + merged @ w_out[l], ln1_g[l], ln1_b[l])
        moe_out = grouped_moe(x, router_w, router_b, moe_w_gate[l], moe_w_up[l], moe_w_down[l])
        x = layer_norm(ALPHA * x + moe_out, ln2_g[l], ln2_b[l])
    return x
```

```python
import functools

import jax
import jax.numpy as jnp
from jax import lax
from jax.experimental import pallas as pl
from jax.experimental.pallas import tpu as pltpu

F32 = jnp.float32
BF16 = jnp.bfloat16

D_MODEL = 1024
DEPTH = 4
HEADS = 8
HEAD_DIM = 64
RW = HEADS * HEAD_DIM
LORA_PAD = 128
RW_COLS = 3 * RW + 4 * LORA_PAD
SGU_W = 256
SGU_CHUNK = 128
SGU_GROUPS = 4
CONV_W = 256
CONV_K = 31
N_EXPERTS = 32
GROUP_SIZE = 8
N_GROUPS = 4
D_EXPERT = 512
EXPERT_BLOCK = 256
ALPHA = (2 * DEPTH) ** 0.25
LN_EPS = 1e-5
GN_EPS = 64e-5

_O_WL = 3 * RW
_O_AL = _O_WL + 32
_O_GL = _O_AL + 32
_O_SGU = _O_GL + 96
_O_CONV = _O_SGU + 2 * SGU_W
_O_GATE = _O_CONV + 2 * CONV_W
_O_VRES = _O_GATE + 3 * D_MODEL

SEQ_TILE = 256
WKV_CHUNK = 64
WKV_TILE = 256
PAIR = 2 * HEAD_DIM
VMEM_LIMIT = 48 * 1024 * 1024


def _dot(a, b):
    return jnp.dot(a, b, preferred_element_type=F32)


def _dot_nt(a, b):
    return lax.dot_general(a, b, (((1,), (1,)), ((), ())), preferred_element_type=F32)


def _bf(x):
    return x.astype(BF16)


def _split_dot(x, w_bf16):
    hi = _bf(x)
    lo = _bf(x - hi.astype(F32))
    return _dot(hi, w_bf16) + _dot(lo, w_bf16)


def _layer_norm(x, g, b):
    mu = jnp.mean(x, axis=-1, keepdims=True)
    d = x - mu
    var = jnp.mean(d * d, axis=-1, keepdims=True)
    return d * lax.rsqrt(var + LN_EPS) * g + b


def _sigmoid(x):
    return 1.0 / (1.0 + jnp.exp(-x))


def _const_spec(shape):
    nd = len(shape)
    return pl.BlockSpec(shape, lambda *_: (0,) * nd, pipeline_mode=pl.Buffered(1))


def _mix_in_body(has_vres, ts, *refs):
    it = iter(refs)
    x_ref, wrw_ref, wsgu_ref, wconv_ref, wgate_ref = (next(it) for _ in range(5))
    mu_ref, w0_ref, wup_ref, a0_ref, aup_ref, gup_ref, kk_ref, ka_ref = (next(it) for _ in range(8))
    if has_vres:
        v0_ref, vup_ref, vfirst_ref = (next(it) for _ in range(3))
    hs_ref = next(it)
    slng_ref, slnb_ref, sw_ref, sbias_ref, swo_ref = (next(it) for _ in range(5))
    cdw_ref, cdb_ref, clng_ref, clnb_ref, cwo_ref = (next(it) for _ in range(5))
    r_o, lw_o, k_o, v_o, a_o, b_o, g_o, ga_o, part_o = (next(it) for _ in range(9))
    pbuf, hext = next(it), next(it)

    @pl.when(pl.program_id(1) == 0)
    def _():
        pbuf[0:8, :] = jnp.zeros((8, RW_COLS), F32)
        hext[0:32, :] = jnp.zeros((32, CONV_W), F32)

    xb = _bf(x_ref[...])

    pr = _dot(xb, wrw_ref[...])
    pbuf[8:8 + ts, :] = pr
    prev = pbuf[pl.ds(7, ts), :]
    pa = pr + (prev - pr) * mu_ref[...]
    pbuf[0:8, :] = pbuf[ts:ts + 8, :]
    r = pa[:, 0:RW]
    k = pa[:, RW:2 * RW]
    v = pa[:, 2 * RW:3 * RW]
    wl = pa[:, 3 * RW:3 * RW + LORA_PAD]
    al = pa[:, 3 * RW + LORA_PAD:3 * RW + 2 * LORA_PAD]
    gl = pa[:, 3 * RW + 2 * LORA_PAD:3 * RW + 3 * LORA_PAD]
    z = w0_ref[...] + _dot(_bf(jnp.tanh(wl)), wup_ref[...])
    nz = -z
    softplus = jnp.maximum(nz, 0.0) + jnp.log(1.0 + jnp.exp(-jnp.abs(nz)))
    lw_o[...] = -jnp.exp(-softplus - 0.5)
    iclr = _sigmoid(a0_ref[...] + _dot(_bf(al), aup_ref[...]))
    g_o[...] = _dot(_bf(_sigmoid(gl)), gup_ref[...])
    kkv = k * kk_ref[...]
    ss = _split_dot(kkv * kkv, hs_ref[...])
    kkn = kkv / jnp.maximum(jnp.sqrt(ss), 1e-12)
    k_o[...] = k * (1.0 + (iclr - 1.0) * ka_ref[...])
    if has_vres:
        vl = pa[:, 3 * RW + 3 * LORA_PAD:RW_COLS]
        v = v + (vfirst_ref[...] - v) * _sigmoid(v0_ref[...] + _dot(_bf(vl), vup_ref[...]))
    r_o[...] = r
    v_o[...] = v
    a_o[...] = -kkn
    b_o[...] = kkn * iclr

    pb = _dot(xb, wsgu_ref[...])
    zz = 0.5 * pb * (1.0 + lax.erf(pb * (2.0 ** -0.5)))
    u = zz[:, :SGU_W]
    vv = _layer_norm(zz[:, SGU_W:], slng_ref[...], slnb_ref[...])
    prow = lax.broadcasted_iota(jnp.int32, (SGU_CHUNK, SGU_GROUPS * SGU_CHUNK), 0)
    qcol = lax.broadcasted_iota(jnp.int32, (SGU_CHUNK, SGU_GROUPS * SGU_CHUNK), 1) % SGU_CHUNK
    wc = _bf(jnp.where(qcol <= prow, sw_ref[...], 0.0))
    lane_grp = lax.broadcasted_iota(jnp.int32, (SGU_CHUNK, SGU_W), 1) // (SGU_W // SGU_GROUPS)
    mixed = []
    for c in range(ts // SGU_CHUNK):
        vc = vv[c * SGU_CHUNK:(c + 1) * SGU_CHUNK]
        stack = jnp.concatenate([jnp.where(lane_grp == g, vc, 0.0) for g in range(SGU_GROUPS)], axis=0)
        mixed.append(_dot(wc, _bf(stack)) + sbias_ref[...])
    mixed = jnp.concatenate(mixed, axis=0)
    o_b = _dot(_bf(u * mixed), swo_ref[...])

    pc = _dot(xb, wconv_ref[...])
    hext[32:32 + ts, :] = pc[:, :CONV_W] * _sigmoid(pc[:, CONV_W:])
    acc = jnp.zeros((ts, CONV_W), F32) + cdb_ref[...]
    for j in range(CONV_K):
        acc = acc + cdw_ref[j:j + 1, :] * hext[pl.ds(32 - (CONV_K - 1) + j, ts), :]
    hext[0:32, :] = hext[ts:ts + 32, :]
    hc = _layer_norm(acc, clng_ref[...], clnb_ref[...])
    hc = hc * _sigmoid(hc)
    o_c = _dot(_bf(hc), cwo_ref[...])

    gates = _sigmoid(_dot(xb, wgate_ref[...]))
    ga_o[...] = gates[:, :D_MODEL]
    part_o[...] = gates[:, D_MODEL:2 * D_MODEL] * o_b + gates[:, 2 * D_MODEL:] * o_c


def _mix_in(x, p, v_first, bsz, seq):
    t = bsz * seq
    ts = SEQ_TILE
    nst = seq // ts
    has_vres = v_first is not None
    row = lambda c: pl.BlockSpec((ts, c), lambda b, s: (b * nst + s, 0))
    args = [x, p['w_rw'], p['w_sgu'], p['w_conv'], p['w_gate'], p['mu'], p['w0'], p['w_up'], p['a0'], p['a_up'],
            p['g_up'], p['k_k'], p['k_a']]
    specs = [row(D_MODEL)] + [_const_spec(a.shape) for a in args[1:]]
    if has_vres:
        args += [p['v0'], p['v_up'], v_first]
        specs += [_const_spec(p['v0'].shape), _const_spec(p['v_up'].shape), row(RW)]
    tail = [p['head_sum'], p['sgu_ln_g'], p['sgu_ln_b'], p['sgu_w'], p['sgu_bias'], p['sgu_w_o'],
            p['conv_dw'], p['conv_db'], p['conv_ln_g'], p['conv_ln_b'], p['conv_w_o']]
    args += tail
    specs += [_const_spec(a.shape) for a in tail]
    out_shape = [jax.ShapeDtypeStruct((t, RW), F32)] * 7 + [jax.ShapeDtypeStruct((t, D_MODEL), F32)] * 2
    out_specs = [row(RW)] * 7 + [row(D_MODEL)] * 2
    return pl.pallas_call(
        functools.partial(_mix_in_body, has_vres, ts),
        out_shape=out_shape,
        grid=(bsz, nst),
        in_specs=specs,
        out_specs=out_specs,
        scratch_shapes=[pltpu.VMEM((ts + 8, RW_COLS), F32), pltpu.VMEM((ts + 32, CONV_W), F32)],
        compiler_params=pltpu.CompilerParams(dimension_semantics=("arbitrary", "arbitrary"),
                                             vmem_limit_bytes=VMEM_LIMIT),
        name="mix_in",
    )(*args)


def _wkv_body(tb, r_ref, lw_ref, k_ref, v_ref, a_ref, b_ref, y_ref, st_ref):
    c = WKV_CHUNK

    @pl.when(pl.program_id(1) == 0)
    def _():
        st_ref[...] = jnp.zeros(st_ref.shape, F32)

    lane = lax.broadcasted_iota(jnp.int32, (c, PAIR), 1)
    head0 = lane < HEAD_DIM
    ti = lax.broadcasted_iota(jnp.int32, (PAIR, PAIR), 0)
    si = lax.broadcasted_iota(jnp.int32, (PAIR, PAIR), 1)
    same = (ti // c) == (si // c)
    lower = same & ((si % c) <= (ti % c))
    strict = same & ((si % c) < (ti % c))
    eye = (ti == si).astype(F32)
    tri = _bf((lax.broadcasted_iota(jnp.int32, (c, c), 1) <= lax.broadcasted_iota(jnp.int32, (c, c), 0)).astype(F32))

    def stack(x):
        return jnp.concatenate([jnp.where(head0, x, 0.0), jnp.where(head0, 0.0, x)], axis=0)

    def chunk(ci, carry):
        rows = pl.ds(pl.multiple_of(ci * c, c), c)
        for p in range(HEADS // 2):
            cols = pl.ds(p * PAIR, PAIR)
            r = r_ref[rows, cols]
            lw = lw_ref[rows, cols]
            k = k_ref[rows, cols]
            v = v_ref[rows, cols]
            a = a_ref[rows, cols]
            b = b_ref[rows, cols]
            cum = _split_dot_left(tri, lw)
            cum_end = cum[c - 1:c, :]
            p_in = jnp.exp(cum)
            p_inv = jnp.exp(-cum)
            p_end = jnp.exp(cum_end - cum)
            r_s = stack(r * p_in)
            a_s = stack(a * jnp.exp(cum - lw))
            k_s = stack(k * p_inv)
            b_s = stack(b * p_inv)
            v_s = stack(v)
            lhs = _bf(jnp.concatenate([r_s, a_s], axis=0))
            rhs = _bf(jnp.concatenate([k_s, b_s], axis=0))
            g = _dot_nt(lhs, rhs)
            a_rk = jnp.where(lower, g[0:PAIR, 0:PAIR], 0.0)
            a_rb = jnp.where(lower, g[0:PAIR, PAIR:], 0.0)
            a_ak = jnp.where(strict, g[PAIR:, 0:PAIR], 0.0)
            a_ab = jnp.where(strict, g[PAIR:, PAIR:], 0.0)
            inv = eye + a_ab
            apow = a_ab
            for _ in range(5):
                apb = _bf(apow)
                apow = _dot(apb, apb)
                inv = inv + _dot(_bf(inv), _bf(apow))
            v_b = _bf(v_s)
            akv = _dot(_bf(a_ak), v_b)
            tw = _dot(_bf(inv), _bf(jnp.concatenate([a_s, akv], axis=1)))
            s0 = st_ref[p]
            s0b = _bf(s0)
            u = _dot(_bf(tw[:, :PAIR]), s0b) + tw[:, PAIR:]
            vu = _bf(jnp.concatenate([v_s, u], axis=0))
            y = _dot(_bf(r_s), s0b) + _dot(_bf(jnp.concatenate([a_rk, a_rb], axis=1)), vu)
            y_ref[rows, cols] = y[0:c] + y[c:]
            kb_t = jnp.concatenate([stack(k * p_end).T, stack(b * p_end).T], axis=1)
            decay_col = jnp.exp(jnp.broadcast_to(cum_end, (PAIR, PAIR))).T
            st_ref[p] = decay_col * s0 + _dot(_bf(kb_t), vu)
        return carry

    lax.fori_loop(0, tb // c, chunk, 0)


def _split_dot_left(w_bf16, x):
    hi = _bf(x)
    lo = _bf(x - hi.astype(F32))
    return _dot(w_bf16, hi) + _dot(w_bf16, lo)


def _wkv(r, lw, k, v, a, b, bsz, seq):
    t = bsz * seq
    tb = WKV_TILE
    nst = seq // tb
    row = pl.BlockSpec((tb, RW), lambda bi, s: (bi * nst + s, 0))
    return pl.pallas_call(
        functools.partial(_wkv_body, tb),
        out_shape=jax.ShapeDtypeStruct((t, RW), F32),
        grid=(bsz, nst),
        in_specs=[row] * 6,
        out_specs=row,
        scratch_shapes=[pltpu.VMEM((HEADS // 2, PAIR, PAIR), F32)],
        compiler_params=pltpu.CompilerParams(dimension_semantics=("arbitrary", "arbitrary"),
                                             vmem_limit_bytes=VMEM_LIMIT),
        name="wkv",
    )(r, lw, k, v, a, b)


def _mix_out_body(y_ref, r_ref, k_ref, v_ref, g_ref, ga_ref, part_ref, x_ref, hs_ref, rk_ref, gng_ref, gnb_ref,
                  wo_ref, wout_ref, ln_g_ref, ln_b_ref, rw_hi_ref, rw_lo_ref, rb_ref, x1_o, route_o):
    hs = hs_ref[...]
    y = y_ref[...]
    inv_n = 1.0 / HEAD_DIM
    mu = _split_dot(y, hs) * inv_n
    d = y - mu
    var = _split_dot(d * d, hs) * inv_n
    yn = d * lax.rsqrt(var + GN_EPS) * gng_ref[...] + gnb_ref[...]
    bonus = _split_dot(r_ref[...] * k_ref[...] * rk_ref[...], hs) * v_ref[...]
    o_a = _dot(_bf((yn + bonus) * g_ref[...]), wo_ref[...])
    merged = ga_ref[...] * o_a + part_ref[...]
    x1 = _layer_norm(ALPHA * x_ref[...] + _dot(_bf(merged), wout_ref[...]), ln_g_ref[...], ln_b_ref[...])
    x1_o[...] = x1

    xh = _bf(x1)
    xl = _bf(x1 - xh.astype(F32))
    logits = _dot(xh, rw_hi_ref[...]) + _dot(xl, rw_hi_ref[...]) + _dot(xh, rw_lo_ref[...])
    scores = _sigmoid(logits)
    sel = scores + rb_ref[...]
    lane = lax.broadcasted_iota(jnp.int32, sel.shape, 1).astype(F32)
    neg = jnp.float32(-jnp.inf)
    best = e0 = e1 = None
    for g in range(N_GROUPS):
        in_g = (lane >= g * GROUP_SIZE) & (lane < (g + 1) * GROUP_SIZE)
        sg = jnp.where(in_g, sel, neg)
        m1 = jnp.max(sg, axis=-1, keepdims=True)
        i1 = jnp.min(jnp.where(sg == m1, lane, 1e9), axis=-1, keepdims=True)
        sg2 = jnp.where(lane == i1, neg, sg)
        m2 = jnp.max(sg2, axis=-1, keepdims=True)
        i2 = jnp.min(jnp.where(sg2 == m2, lane, 1e9), axis=-1, keepdims=True)
        gs = m1 + m2
        if g == 0:
            best, e0, e1 = gs, i1, i2
        else:
            take = gs > best
            best = jnp.where(take, gs, best)
            e0 = jnp.where(take, i1, e0)
            e1 = jnp.where(take, i2, e1)
    g0 = jnp.sum(jnp.where(lane == e0, scores, 0.0), axis=-1, keepdims=True)
    g1 = jnp.sum(jnp.where(lane == e1, scores, 0.0), axis=-1, keepdims=True)
    tot = g0 + g1
    route_o[...] = jnp.where(lane == 0, e0, jnp.where(lane == 1, e1, jnp.where(lane == 2, g0 / tot,
                             jnp.where(lane == 3, g1 / tot, 0.0))))


def _mix_out(y, r, k, v, g, ga, part, x, p, bsz, seq):
    t = bsz * seq
    ts = SEQ_TILE
    row = lambda c: pl.BlockSpec((ts, c), lambda i: (i, 0))
    consts = [p['head_sum'], p['r_k'], p['gn_g'], p['gn_b'], p['rwkv_w_o'], p['w_out'], p['ln1_g'], p['ln1_b'],
              p['router_hi'], p['router_lo'], p['router_b']]
    return pl.pallas_call(
        _mix_out_body,
        out_shape=[jax.ShapeDtypeStruct((t, D_MODEL), F32), jax.ShapeDtypeStruct((t, 128), F32)],
        grid=(t // ts,),
        in_specs=[row(RW)] * 5 + [row(D_MODEL)] * 3 + [_const_spec(a.shape) for a in consts],
        out_specs=[row(D_MODEL), row(128)],
        compiler_params=pltpu.CompilerParams(dimension_semantics=("arbitrary",), vmem_limit_bytes=VMEM_LIMIT),
        name="mix_out",
    )(y, r, k, v, g, ga, part, x, *consts)


def _experts_body(be_ref, nb_ref, x_ref, wg_ref, wu_ref, wd_ref, o_ref):
    i = pl.program_id(0)

    @pl.when(i < nb_ref[0])
    def _():
        xb = x_ref[...]
        hg = _dot(xb, wg_ref[...])
        h = hg * _sigmoid(hg) * _dot(xb, wu_ref[...])
        o_ref[...] = _dot(_bf(h), wd_ref[...])

    @pl.when(i >= nb_ref[0])
    def _():
        o_ref[...] = jnp.zeros(o_ref.shape, F32)


def _experts(xg, block_expert, n_used, wg, wu, wd):
    rows = xg.shape[0]
    nb = rows // EXPERT_BLOCK
    grid_spec = pltpu.PrefetchScalarGridSpec(
        num_scalar_prefetch=2,
        grid=(nb,),
        in_specs=[
            pl.BlockSpec((EXPERT_BLOCK, D_MODEL), lambda i, be, nu: (i, 0)),
            pl.BlockSpec((None, D_MODEL, D_EXPERT), lambda i, be, nu: (be[i], 0, 0)),
            pl.BlockSpec((None, D_MODEL, D_EXPERT), lambda i, be, nu: (be[i], 0, 0)),
            pl.BlockSpec((None, D_EXPERT, D_MODEL), lambda i, be, nu: (be[i], 0, 0)),
        ],
        out_specs=pl.BlockSpec((EXPERT_BLOCK, D_MODEL), lambda i, be, nu: (i, 0)),
    )
    return pl.pallas_call(
        _experts_body,
        out_shape=jax.ShapeDtypeStruct((rows, D_MODEL), F32),
        grid_spec=grid_spec,
        compiler_params=pltpu.CompilerParams(dimension_semantics=("arbitrary",), vmem_limit_bytes=VMEM_LIMIT),
        name="experts",
    )(block_expert, n_used, xg, wg, wu, wd)


def _combine_body(x_ref, y0_ref, y1_ref, route_ref, g_ref, b_ref, o_ref):
    route = route_ref[...]
    moe = route[:, 2:3] * y0_ref[...] + route[:, 3:4] * y1_ref[...]
    o_ref[...] = _layer_norm(ALPHA * x_ref[...] + moe, g_ref[...], b_ref[...])


def _combine(x1, y0, y1, route, ln_g, ln_b):
    t = x1.shape[0]
    ts = SEQ_TILE
    row = lambda c: pl.BlockSpec((ts, c), lambda i: (i, 0))
    return pl.pallas_call(
        _combine_body,
        out_shape=jax.ShapeDtypeStruct((t, D_MODEL), F32),
        grid=(t // ts,),
        in_specs=[row(D_MODEL)] * 3 + [row(128), _const_spec(ln_g.shape), _const_spec(ln_b.shape)],
        out_specs=row(D_MODEL),
        compiler_params=pltpu.CompilerParams(dimension_semantics=("arbitrary",), vmem_limit_bytes=VMEM_LIMIT),
        name="combine",
    )(x1, y0, y1, route, ln_g, ln_b)


def _dispatch(route, t):
    expert = route[:, 0:2].astype(jnp.int32).reshape(2 * t)
    onehot = (expert[:, None] == jnp.arange(N_EXPERTS, dtype=jnp.int32)[None, :]).astype(jnp.int32)
    rank = jnp.sum((jnp.cumsum(onehot, axis=0) - onehot) * onehot, axis=1)
    counts = jnp.sum(onehot, axis=0)
    padded = (counts + EXPERT_BLOCK - 1) // EXPERT_BLOCK * EXPERT_BLOCK
    pad_end = jnp.cumsum(padded)
    dest = (pad_end - padded)[expert] + rank
    n_blocks = (2 * t + EXPERT_BLOCK - 1) // EXPERT_BLOCK + N_EXPERTS
    tok = jnp.arange(2 * t, dtype=jnp.int32) // 2
    buf_tok = jnp.full((n_blocks * EXPERT_BLOCK,), t, jnp.int32).at[dest].set(tok)
    block_expert = jnp.clip(
        jnp.searchsorted(pad_end, jnp.arange(n_blocks, dtype=jnp.int32) * EXPERT_BLOCK, side='right'),
        0, N_EXPERTS - 1).astype(jnp.int32)
    n_used = (pad_end[-1:] // EXPERT_BLOCK).astype(jnp.int32)
    return buf_tok, dest.reshape(t, 2), block_expert, n_used


def _pad_cols(w, width=LORA_PAD):
    return jnp.pad(w, ((0, 0), (0, width - w.shape[1])))


def _pad_rows(w, height=LORA_PAD):
    return jnp.pad(w, ((0, height - w.shape[0]), (0, 0)))


def _row(v):
    return v.reshape(1, -1).astype(F32)


def _layer_params(l, a):
    w_in = a['w_in_first'] if l == 0 else a['w_in_rest'][l - 1]
    if l == 0:
        w_vres = jnp.zeros((D_MODEL, LORA_PAD), F32)
        mu_vres = jnp.zeros((LORA_PAD,), F32)
    else:
        w_vres = _pad_cols(w_in[:, _O_VRES:])
        mu_vres = jnp.pad(a['rwkv_mu_vres'][l - 1], (0, LORA_PAD - 32))
    mu = a['rwkv_mu'][l]
    p = {
        'w_rw': _bf(jnp.concatenate([w_in[:, :_O_WL], _pad_cols(w_in[:, _O_WL:_O_AL]), _pad_cols(w_in[:, _O_AL:_O_GL]),
                                     _pad_cols(w_in[:, _O_GL:_O_SGU]), w_vres], axis=1)),
        'w_sgu': _bf(w_in[:, _O_SGU:_O_CONV]),
        'w_conv': _bf(w_in[:, _O_CONV:_O_GATE]),
        'w_gate': _bf(w_in[:, _O_GATE:_O_VRES]),
        'mu': _row(jnp.concatenate([mu[:_O_WL], jnp.pad(mu[_O_WL:_O_AL], (0, 96)), jnp.pad(mu[_O_AL:_O_GL], (0, 96)),
                                    jnp.pad(mu[_O_GL:], (0, 32)), mu_vres])),
        'w0': _row(a['rwkv_w0'][l]),
        'w_up': _bf(_pad_rows(a['rwkv_w_up'][l])),
        'a0': _row(a['rwkv_a0'][l]),
        'a_up': _bf(_pad_rows(a['rwkv_a_up'][l])),
        'g_up': _bf(_pad_rows(a['rwkv_g_up'][l])),
        'k_k': _row(a['rwkv_k_k'][l]),
        'k_a': _row(a['rwkv_k_a'][l]),
        'r_k': _row(a['rwkv_r_k'][l]),
        'gn_g': _row(a['rwkv_gn_g'][l]),
        'gn_b': _row(a['rwkv_gn_b'][l]),
        'rwkv_w_o': _bf(a['rwkv_w_o'][l]),
        'sgu_ln_g': _row(a['sgu_ln_g'][l]),
        'sgu_ln_b': _row(a['sgu_ln_b'][l]),
        'sgu_w': jnp.transpose(a['sgu_w'][l], (1, 0, 2)).reshape(SGU_CHUNK, SGU_GROUPS * SGU_CHUNK),
        'sgu_bias': jnp.repeat(jnp.transpose(a['sgu_b'][l]), SGU_W // SGU_GROUPS, axis=1),
        'sgu_w_o': _bf(a['sgu_w_o'][l]),
        'conv_dw': jnp.pad(a['conv_dw'][l], ((0, 1), (0, 0))),
        'conv_db': _row(a['conv_db'][l]),
        'conv_ln_g': _row(a['conv_ln_g'][l]),
        'conv_ln_b': _row(a['conv_ln_b'][l]),
        'conv_w_o': _bf(a['conv_w_o'][l]),
        'w_out': _bf(a['w_out'][l]),
        'ln1_g': _row(a['ln1_g'][l]),
        'ln1_b': _row(a['ln1_b'][l]),
        'ln2_g': _row(a['ln2_g'][l]),
        'ln2_b': _row(a['ln2_b'][l]),
    }
    if l > 0:
        p['v0'] = _row(a['rwkv_v0'][l - 1])
        p['v_up'] = _bf(_pad_rows(a['rwkv_v_up'][l - 1]))
    return p


def _forward(a):
    x = a['x']
    bsz, seq, _ = x.shape
    t = bsz * seq
    x = x.reshape(t, D_MODEL)
    head = jnp.arange(RW, dtype=jnp.int32) // HEAD_DIM
    head_sum = _bf((head[:, None] == head[None, :]).astype(F32))
    rw = _pad_cols(a['router_w'])
    rw_hi = _bf(rw)
    rw_lo = _bf(rw - rw_hi.astype(F32))
    rb = _row(jnp.pad(a['router_b'], (0, LORA_PAD - N_EXPERTS)))
    v_first = None
    for l in range(DEPTH):
        p = _layer_params(l, a)
        p.update(head_sum=head_sum, router_hi=rw_hi, router_lo=rw_lo, router_b=rb)
        r, lw, k, v, na, nb, g, ga, part = _mix_in(x, p, v_first, bsz, seq)
        if l == 0:
            v_first = v
        y = _wkv(r, lw, k, v, na, nb, bsz, seq)
        x1, route = _mix_out(y, r, k, v, g, ga, part, x, p, bsz, seq)
        buf_tok, dest, block_expert, n_used = _dispatch(route, t)
        xg = jnp.take(_bf(x1), buf_tok, axis=0, mode='fill', fill_value=0)
        yb = _experts(xg, block_expert, n_used, _bf(a['moe_w_gate'][l]), _bf(a['moe_w_up'][l]),
                      _bf(a['moe_w_down'][l]))
        y0 = jnp.take(yb, dest[:, 0], axis=0)
        y1 = jnp.take(yb, dest[:, 1], axis=0)
        x = _combine(x1, y0, y1, route, p['ln2_g'], p['ln2_b'])
    return x.reshape(bsz, seq, D_MODEL)


def kernel(x, w_in_first, w_in_rest, rwkv_mu, rwkv_mu_vres, rwkv_w0, rwkv_w_up, rwkv_a0, rwkv_a_up, rwkv_g_up, rwkv_v0, rwkv_v_up, rwkv_k_k, rwkv_k_a, rwkv_r_k, rwkv_gn_g, rwkv_gn_b, rwkv_w_o, sgu_ln_g, sgu_ln_b, sgu_w, sgu_b, sgu_w_o, conv_dw, conv_db, conv_ln_g, conv_ln_b, conv_w_o, w_out, ln1_g, ln1_b, router_w, router_b, moe_w_gate, moe_w_up, moe_w_down, ln2_g, ln2_b):
    return _forward(dict(
        x=x, w_in_first=w_in_first, w_in_rest=w_in_rest, rwkv_mu=rwkv_mu, rwkv_mu_vres=rwkv_mu_vres,
        rwkv_w0=rwkv_w0, rwkv_w_up=rwkv_w_up, rwkv_a0=rwkv_a0, rwkv_a_up=rwkv_a_up, rwkv_g_up=rwkv_g_up,
        rwkv_v0=rwkv_v0, rwkv_v_up=rwkv_v_up, rwkv_k_k=rwkv_k_k, rwkv_k_a=rwkv_k_a, rwkv_r_k=rwkv_r_k,
        rwkv_gn_g=rwkv_gn_g, rwkv_gn_b=rwkv_gn_b, rwkv_w_o=rwkv_w_o, sgu_ln_g=sgu_ln_g, sgu_ln_b=sgu_ln_b,
        sgu_w=sgu_w, sgu_b=sgu_b, sgu_w_o=sgu_w_o, conv_dw=conv_dw, conv_db=conv_db, conv_ln_g=conv_ln_g,
        conv_ln_b=conv_ln_b, conv_w_o=conv_w_o, w_out=w_out, ln1_g=ln1_g, ln1_b=ln1_b, router_w=router_w,
        router_b=router_b, moe_w_gate=moe_w_gate, moe_w_up=moe_w_up, moe_w_down=moe_w_down, ln2_g=ln2_g,
        ln2_b=ln2_b))
```

```python
import functools

import jax
import jax.numpy as jnp
from jax import lax
from jax.experimental import pallas as pl
from jax.experimental.pallas import tpu as pltpu

F32 = jnp.float32
BF16 = jnp.bfloat16

D_MODEL = 1024
DEPTH = 4
HEADS = 8
HEAD_DIM = 64
RW = HEADS * HEAD_DIM
LORA_PAD = 128
RW_COLS = 3 * RW + 4 * LORA_PAD
SGU_W = 256
SGU_CHUNK = 128
SGU_GROUPS = 4
CONV_W = 256
CONV_K = 31
N_EXPERTS = 32
GROUP_SIZE = 8
N_GROUPS = 4
D_EXPERT = 512
EXPERT_BLOCK = 256
ALPHA = (2 * DEPTH) ** 0.25
LN_EPS = 1e-5
GN_EPS = 64e-5

_O_WL = 3 * RW
_O_AL = _O_WL + 32
_O_GL = _O_AL + 32
_O_SGU = _O_GL + 96
_O_CONV = _O_SGU + 2 * SGU_W
_O_GATE = _O_CONV + 2 * CONV_W
_O_VRES = _O_GATE + 3 * D_MODEL

SEQ_TILE = 256
WKV_CHUNK = 64
WKV_TILE = 256
MIX_OUT_HALVES = 2
PAIR = 2 * HEAD_DIM
VMEM_LIMIT = 48 * 1024 * 1024


def _dot(a, b):
    return jnp.dot(a, b, preferred_element_type=F32)


def _dot_nt(a, b):
    return lax.dot_general(a, b, (((1,), (1,)), ((), ())), preferred_element_type=F32)


def _bf(x):
    return x.astype(BF16)


def _split_dot(x, w_bf16):
    hi = _bf(x)
    lo = _bf(x - hi.astype(F32))
    return _dot(hi, w_bf16) + _dot(lo, w_bf16)


def _split_dot_left(w_bf16, x):
    hi = _bf(x)
    lo = _bf(x - hi.astype(F32))
    return _dot(w_bf16, hi) + _dot(w_bf16, lo)


def _layer_norm(x, g, b):
    mu = jnp.mean(x, axis=-1, keepdims=True)
    d = x - mu
    var = jnp.mean(d * d, axis=-1, keepdims=True)
    return d * lax.rsqrt(var + LN_EPS) * g + b


def _sigmoid(x):
    return 1.0 / (1.0 + jnp.exp(-x))


def _const_spec(shape):
    nd = len(shape)
    return pl.BlockSpec(shape, lambda *_: (0,) * nd, pipeline_mode=pl.Buffered(1))


def _mix_in_body(has_vres, ts, *refs):
    it = iter(refs)
    x_ref, wrw_ref, wsgu_ref, wconv_ref, wgate_ref = (next(it) for _ in range(5))
    mu_ref, w0_ref, wup_ref, a0_ref, aup_ref, gup_ref, kk_ref, ka_ref = (next(it) for _ in range(8))
    if has_vres:
        v0_ref, vup_ref, vfirst_ref = (next(it) for _ in range(3))
    hs_ref = next(it)
    slng_ref, slnb_ref, sw_ref, sbias_ref, swo_ref = (next(it) for _ in range(5))
    cdw_ref, cdb_ref, clng_ref, clnb_ref, cwo_ref = (next(it) for _ in range(5))
    r_o, lw_o, k_o, v_o, a_o, b_o, g_o, ga_o, part_o = (next(it) for _ in range(9))
    pbuf, hext, hshift = next(it), next(it), next(it)

    @pl.when(pl.program_id(1) == 0)
    def _():
        pbuf[0:8, :] = jnp.zeros((8, RW_COLS), F32)
        hext[0:32, :] = jnp.zeros((32, CONV_W), F32)

    xb = _bf(x_ref[...])
    pr = _dot(xb, wrw_ref[...])
    pb = _dot(xb, wsgu_ref[...])
    pc = _dot(xb, wconv_ref[...])
    gate_logits = _dot(xb, wgate_ref[...])

    pbuf[8:8 + ts, :] = pr
    prev = pbuf[pl.ds(7, ts), :]
    pa = pr + (prev - pr) * mu_ref[...]
    pbuf[0:8, :] = pbuf[ts:ts + 8, :]
    r = pa[:, 0:RW]
    k = pa[:, RW:2 * RW]
    v = pa[:, 2 * RW:3 * RW]
    wl = pa[:, 3 * RW:3 * RW + LORA_PAD]
    al = pa[:, 3 * RW + LORA_PAD:3 * RW + 2 * LORA_PAD]
    gl = pa[:, 3 * RW + 2 * LORA_PAD:3 * RW + 3 * LORA_PAD]
    z = w0_ref[...] + _dot(_bf(jnp.tanh(wl)), wup_ref[...])
    nz = -z
    softplus = jnp.maximum(nz, 0.0) + jnp.log(1.0 + jnp.exp(-jnp.abs(nz)))
    lw_o[...] = -jnp.exp(-softplus - 0.5)
    iclr = _sigmoid(a0_ref[...] + _dot(_bf(al), aup_ref[...]))
    g_o[...] = _dot(_bf(_sigmoid(gl)), gup_ref[...])
    kkv = k * kk_ref[...]
    ss = _dot(_bf(kkv * kkv), hs_ref[...])
    kkn = kkv / jnp.maximum(jnp.sqrt(ss), 1e-12)
    k_o[...] = k * (1.0 + (iclr - 1.0) * ka_ref[...])
    if has_vres:
        vl = pa[:, 3 * RW + 3 * LORA_PAD:RW_COLS]
        v = v + (vfirst_ref[...] - v) * _sigmoid(v0_ref[...] + _dot(_bf(vl), vup_ref[...]))
    r_o[...] = r
    v_o[...] = v
    a_o[...] = -kkn
    b_o[...] = kkn * iclr

    zz = 0.5 * pb * (1.0 + lax.erf(pb * (2.0 ** -0.5)))
    u = zz[:, :SGU_W]
    vv = _layer_norm(zz[:, SGU_W:], slng_ref[...], slnb_ref[...])
    prow = lax.broadcasted_iota(jnp.int32, (SGU_CHUNK, SGU_GROUPS * SGU_CHUNK), 0)
    qcol = lax.broadcasted_iota(jnp.int32, (SGU_CHUNK, SGU_GROUPS * SGU_CHUNK), 1) % SGU_CHUNK
    wc = _bf(jnp.where(qcol <= prow, sw_ref[...], 0.0))
    lane_grp = lax.broadcasted_iota(jnp.int32, (SGU_CHUNK, SGU_W), 1) // (SGU_W // SGU_GROUPS)
    mixed = []
    for c in range(ts // SGU_CHUNK):
        vc = vv[c * SGU_CHUNK:(c + 1) * SGU_CHUNK]
        stack = jnp.concatenate([jnp.where(lane_grp == g, vc, 0.0) for g in range(SGU_GROUPS)], axis=0)
        mixed.append(_dot(wc, _bf(stack)) + sbias_ref[...])
    mixed = jnp.concatenate(mixed, axis=0)
    o_b = _dot(_bf(u * mixed), swo_ref[...])

    hext[32:32 + ts, :] = pc[:, :CONV_W] * _sigmoid(pc[:, CONV_W:])
    acc = jnp.zeros((ts, CONV_W), F32) + cdb_ref[...]
    first = 32 - (CONV_K - 1)
    for rem in range(8):
        taps = [j for j in range(CONV_K) if (first + j) % 8 == rem]
        span = max(first + j for j in taps) - rem
        hshift[0:span + ts, :] = hext[pl.ds(rem, span + ts), :]
        for j in taps:
            off = first + j - rem
            acc = acc + cdw_ref[j:j + 1, :] * hshift[off:off + ts, :]
    hext[0:32, :] = hext[ts:ts + 32, :]
    hc = _layer_norm(acc, clng_ref[...], clnb_ref[...])
    hc = hc * _sigmoid(hc)
    o_c = _dot(_bf(hc), cwo_ref[...])

    gates = _sigmoid(gate_logits)
    ga_o[...] = gates[:, :D_MODEL]
    part_o[...] = gates[:, D_MODEL:2 * D_MODEL] * o_b + gates[:, 2 * D_MODEL:] * o_c


def _mix_in(x, p, v_first, bsz, seq):
    t = bsz * seq
    ts = SEQ_TILE
    nst = seq // ts
    has_vres = v_first is not None
    row = lambda c: pl.BlockSpec((ts, c), lambda b, s: (b * nst + s, 0))
    args = [x, p['w_rw'], p['w_sgu'], p['w_conv'], p['w_gate'], p['mu'], p['w0'], p['w_up'], p['a0'], p['a_up'],
            p['g_up'], p['k_k'], p['k_a']]
    specs = [row(D_MODEL)] + [_const_spec(a.shape) for a in args[1:]]
    if has_vres:
        args += [p['v0'], p['v_up'], v_first]
        specs += [_const_spec(p['v0'].shape), _const_spec(p['v_up'].shape), row(RW)]
    tail = [p['head_sum'], p['sgu_ln_g'], p['sgu_ln_b'], p['sgu_w'], p['sgu_bias'], p['sgu_w_o'],
            p['conv_dw'], p['conv_db'], p['conv_ln_g'], p['conv_ln_b'], p['conv_w_o']]
    args += tail
    specs += [_const_spec(a.shape) for a in tail]
    out_shape = [jax.ShapeDtypeStruct((t, RW), F32)] * 7 + [jax.ShapeDtypeStruct((t, D_MODEL), F32)] * 2
    out_specs = [row(RW)] * 7 + [row(D_MODEL)] * 2
    return pl.pallas_call(
        functools.partial(_mix_in_body, has_vres, ts),
        out_shape=out_shape,
        grid=(bsz, nst),
        in_specs=specs,
        out_specs=out_specs,
        scratch_shapes=[pltpu.VMEM((ts + 8, RW_COLS), F32), pltpu.VMEM((ts + 32, CONV_W), F32),
                        pltpu.VMEM((ts + 32, CONV_W), F32)],
        compiler_params=pltpu.CompilerParams(dimension_semantics=("arbitrary", "arbitrary"),
                                             vmem_limit_bytes=VMEM_LIMIT),
        name="mix_in",
    )(*args)


def _wkv_body(tb, r_ref, lw_ref, k_ref, v_ref, a_ref, b_ref, y_ref, st_ref, wr_s, ar_s, vs_s, kb_s, dc_s):
    c = WKV_CHUNK
    nc = tb // c
    npair = HEADS // 2
    units = [(ci, p) for ci in range(nc) for p in range(npair)]

    @pl.when(pl.program_id(1) == 0)
    def _():
        st_ref[...] = jnp.zeros(st_ref.shape, F32)

    lane = lax.broadcasted_iota(jnp.int32, (c, PAIR), 1)
    head0 = lane < HEAD_DIM
    ti = lax.broadcasted_iota(jnp.int32, (PAIR, PAIR), 0)
    si = lax.broadcasted_iota(jnp.int32, (PAIR, PAIR), 1)
    same = (ti // c) == (si // c)
    lower = same & ((si % c) <= (ti % c))
    strict = same & ((si % c) < (ti % c))
    eye = (ti == si).astype(F32)
    tr = lax.broadcasted_iota(jnp.int32, (tb, tb), 0)
    tc = lax.broadcasted_iota(jnp.int32, (tb, tb), 1)
    tri = _bf(((tr // c == tc // c) & (tc <= tr)).astype(F32))

    def stack(x):
        return jnp.concatenate([jnp.where(head0, x, 0.0), jnp.where(head0, 0.0, x)], axis=0)

    lw = lw_ref[...]
    cum = _split_dot_left(tri, lw)
    cum_end = jnp.concatenate(
        [jnp.broadcast_to(cum[(ci + 1) * c - 1:(ci + 1) * c, :], (c, RW)) for ci in range(nc)], axis=0)
    p_inv = jnp.exp(-cum)
    p_end = jnp.exp(cum_end - cum)
    rt = r_ref[...] * jnp.exp(cum)
    at = a_ref[...] * jnp.exp(cum - lw)
    kt = k_ref[...] * p_inv
    bt = b_ref[...] * p_inv
    ke = k_ref[...] * p_end
    be = b_ref[...] * p_end
    vt = v_ref[...]

    def sub(x, ci, p):
        return x[ci * c:(ci + 1) * c, p * PAIR:(p + 1) * PAIR]

    a_s = [stack(sub(at, ci, p)) for ci, p in units]
    r_s = [stack(sub(rt, ci, p)) for ci, p in units]
    g = [_dot_nt(_bf(jnp.concatenate([r_s[i], a_s[i]], axis=0)),
                 _bf(jnp.concatenate([stack(sub(kt, ci, p)), stack(sub(bt, ci, p))], axis=0)))
         for i, (ci, p) in enumerate(units)]
    for i, (ci, p) in enumerate(units):
        vs_s[ci, p] = _bf(stack(sub(vt, ci, p)))
        ar_s[ci, p] = _bf(jnp.concatenate([jnp.where(lower, g[i][0:PAIR, 0:PAIR], 0.0),
                                           jnp.where(lower, g[i][0:PAIR, PAIR:], 0.0)], axis=1))
        kb_s[ci, p] = _bf(jnp.concatenate([stack(sub(ke, ci, p)).T, stack(sub(be, ci, p)).T], axis=1))
        dc_s[ci, p] = jnp.exp(jnp.broadcast_to(sub(cum_end, ci, p)[0:1, :], (PAIR, PAIR))).T
    apow = [jnp.where(strict, g[i][PAIR:, PAIR:], 0.0) for i in range(len(units))]
    inv = [eye + x for x in apow]
    apb = [_bf(x) for x in apow]
    apow = [_dot(x, x) for x in apb]
    for _ in range(4):
        apb = [_bf(x) for x in apow]
        both = [_dot(x, jnp.concatenate([x, _bf(y)], axis=1)) for x, y in zip(apb, inv)]
        apow = [x[:, :PAIR] for x in both]
        inv = [y + x[:, PAIR:] for x, y in zip(both, inv)]
    inv = [y + _dot(_bf(x), _bf(y)) for x, y in zip(apow, inv)]
    zero = jnp.zeros((PAIR, PAIR), BF16)
    for i, (ci, p) in enumerate(units):
        a_ak = jnp.where(strict, g[i][PAIR:, 0:PAIR], 0.0)
        tw = _bf(_dot(_bf(inv[i]), _bf(jnp.concatenate([a_s[i], a_ak], axis=1))))
        wr_s[ci, p] = jnp.concatenate([tw, jnp.concatenate([_bf(r_s[i]), zero], axis=1)], axis=0)

    s = [st_ref[p] for p in range(npair)]
    for ci in range(nc):
        ur = [_dot(wr_s[ci, p], jnp.concatenate([_bf(s[p]), vs_s[ci, p]], axis=0)) for p in range(npair)]
        vu = [jnp.concatenate([vs_s[ci, p], _bf(ur[p][:PAIR])], axis=0) for p in range(npair)]
        s = [dc_s[ci, p] * s[p] + _dot(kb_s[ci, p], vu[p]) for p in range(npair)]
        for p in range(npair):
            y = ur[p][PAIR:] + _dot(ar_s[ci, p], vu[p])
            y_ref[ci * c:(ci + 1) * c, p * PAIR:(p + 1) * PAIR] = y[0:c] + y[c:]
    for p in range(npair):
        st_ref[p] = s[p]


def _wkv(r, lw, k, v, a, b, bsz, seq):
    t = bsz * seq
    tb = WKV_TILE
    nst = seq // tb
    nc, npair = tb // WKV_CHUNK, HEADS // 2
    row = pl.BlockSpec((tb, RW), lambda bi, s: (bi * nst + s, 0))
    return pl.pallas_call(
        functools.partial(_wkv_body, tb),
        out_shape=jax.ShapeDtypeStruct((t, RW), F32),
        grid=(bsz, nst),
        in_specs=[row] * 6,
        out_specs=row,
        scratch_shapes=[pltpu.VMEM((npair, PAIR, PAIR), F32),
                        pltpu.VMEM((nc, npair, 2 * PAIR, 2 * PAIR), BF16),
                        pltpu.VMEM((nc, npair, PAIR, 2 * PAIR), BF16),
                        pltpu.VMEM((nc, npair, PAIR, PAIR), BF16),
                        pltpu.VMEM((nc, npair, PAIR, 2 * PAIR), BF16),
                        pltpu.VMEM((nc, npair, PAIR, PAIR), F32)],
        compiler_params=pltpu.CompilerParams(dimension_semantics=("arbitrary", "arbitrary"),
                                             vmem_limit_bytes=VMEM_LIMIT),
        name="wkv",
    )(r, lw, k, v, a, b)


def _mix_out_body(halves, y_ref, r_ref, k_ref, v_ref, g_ref, ga_ref, part_ref, x_ref, hs_ref, rk_ref, gng_ref,
                  gnb_ref, wo_ref, wout_ref, ln_g_ref, ln_b_ref, rw_hi_ref, rw_lo_ref, rb_ref, x1_o, x1b_o, route_o,
                  counts_o, cnt_ref):
    th = y_ref.shape[0] // halves
    hrows = [pl.ds(h * th, th) for h in range(halves)]

    def each(fn, *cols):
        return [fn(*a) for a in zip(*cols)]

    def load(ref):
        return [ref[rs, :] for rs in hrows]

    hs = hs_ref[...]
    inv_n = 1.0 / HEAD_DIM
    y = load(y_ref)
    mu = each(lambda a: _split_dot(a, hs) * inv_n, y)
    d = each(lambda a, m: a - m, y, mu)
    var = each(lambda a: _dot(_bf(a * a), hs) * inv_n, d)
    rkk = each(lambda a, b: _dot(_bf(a * b * rk_ref[...]), hs), load(r_ref), load(k_ref))
    yn = each(lambda a, s2: a * lax.rsqrt(s2 + GN_EPS) * gng_ref[...] + gnb_ref[...], d, var)
    out = each(lambda a, bo, vv, gg: _bf((a + bo * vv) * gg), yn, rkk, load(v_ref), load(g_ref))
    o_a = each(lambda a: _dot(a, wo_ref[...]), out)
    merged = each(lambda a, b, c2: _bf(a * b + c2), load(ga_ref), o_a, load(part_ref))
    xm = each(lambda a: _dot(a, wout_ref[...]), merged)
    x1 = each(lambda a, b: _layer_norm(ALPHA * a + b, ln_g_ref[...], ln_b_ref[...]), load(x_ref), xm)
    xh = each(_bf, x1)
    for rs, a, b in zip(hrows, x1, xh):
        x1_o[rs, :] = a
        x1b_o[rs, :] = b

    xl = each(lambda a, b: _bf(a - b.astype(F32)), x1, xh)
    logits = each(lambda a, b: _dot(a, rw_hi_ref[...]) + _dot(b, rw_hi_ref[...]) + _dot(a, rw_lo_ref[...]), xh, xl)
    scores = each(_sigmoid, logits)
    sel = each(lambda a: a + rb_ref[...], scores)
    lane = lax.broadcasted_iota(jnp.int32, (th, 128), 1).astype(F32)
    neg = jnp.float32(-jnp.inf)

    def top2(sg):
        m1 = jnp.max(sg, axis=-1, keepdims=True)
        i1 = jnp.min(jnp.where(sg == m1, lane, 1e9), axis=-1, keepdims=True)
        sg2 = jnp.where(lane == i1, neg, sg)
        m2 = jnp.max(sg2, axis=-1, keepdims=True)
        i2 = jnp.min(jnp.where(sg2 == m2, lane, 1e9), axis=-1, keepdims=True)
        return m1 + m2, i1, i2

    best = e0 = e1 = None
    for g in range(N_GROUPS):
        in_g = (lane >= g * GROUP_SIZE) & (lane < (g + 1) * GROUP_SIZE)
        cand = each(lambda a: top2(jnp.where(in_g, a, neg)), sel)
        if g == 0:
            best, e0, e1 = ([c[i] for c in cand] for i in range(3))
        else:
            take = each(lambda c, b: c[0] > b, cand, best)
            best = each(lambda tk, c, b: jnp.where(tk, c[0], b), take, cand, best)
            e0 = each(lambda tk, c, b: jnp.where(tk, c[1], b), take, cand, e0)
            e1 = each(lambda tk, c, b: jnp.where(tk, c[2], b), take, cand, e1)
    g0 = each(lambda e, sc: jnp.sum(jnp.where(lane == e, sc, 0.0), axis=-1, keepdims=True), e0, scores)
    g1 = each(lambda e, sc: jnp.sum(jnp.where(lane == e, sc, 0.0), axis=-1, keepdims=True), e1, scores)

    @pl.when(pl.program_id(0) == 0)
    def _():
        cnt_ref[...] = jnp.zeros(cnt_ref.shape, F32)

    chosen = each(lambda a, b: jnp.where((lane == a) | (lane == b), 1.0, 0.0), e0, e1)
    before = _bf((lax.broadcasted_iota(jnp.int32, (th, th), 1)
                  < lax.broadcasted_iota(jnp.int32, (th, th), 0)).astype(F32))
    inside = each(lambda a: _dot(before, _bf(a)), chosen)
    total = each(lambda a: jnp.sum(a, axis=0, keepdims=True), chosen)
    base = cnt_ref[...]
    for h in range(halves):
        seen = inside[h] + base
        rank0 = jnp.sum(jnp.where(lane == e0[h], seen, 0.0), axis=-1, keepdims=True)
        rank1 = jnp.sum(jnp.where(lane == e1[h], seen, 0.0), axis=-1, keepdims=True)
        tot = g0[h] + g1[h]
        route_o[hrows[h], :] = jnp.where(
            lane == 0, e0[h], jnp.where(lane == 1, e1[h], jnp.where(lane == 2, g0[h] / tot, jnp.where(
                lane == 3, g1[h] / tot, jnp.where(lane == 4, rank0, jnp.where(lane == 5, rank1, 0.0))))))
        base = base + total[h]
    cnt_ref[...] = base
    counts_o[...] = base


def _mix_out(y, r, k, v, g, ga, part, x, p, bsz, seq):
    t = bsz * seq
    ts = MIX_OUT_HALVES * SEQ_TILE
    row = lambda c: pl.BlockSpec((ts, c), lambda i: (i, 0))
    consts = [p['head_sum'], p['r_k'], p['gn_g'], p['gn_b'], p['rwkv_w_o'], p['w_out'], p['ln1_g'], p['ln1_b'],
              p['router_hi'], p['router_lo'], p['router_b']]
    return pl.pallas_call(
        functools.partial(_mix_out_body, MIX_OUT_HALVES),
        out_shape=[jax.ShapeDtypeStruct((t, D_MODEL), F32), jax.ShapeDtypeStruct((t, D_MODEL), BF16),
                   jax.ShapeDtypeStruct((t, 128), F32), jax.ShapeDtypeStruct((1, 128), F32)],
        grid=(t // ts,),
        in_specs=[row(RW)] * 5 + [row(D_MODEL)] * 3 + [_const_spec(a.shape) for a in consts],
        out_specs=[row(D_MODEL), row(D_MODEL), row(128), pl.BlockSpec((1, 128), lambda i: (0, 0))],
        scratch_shapes=[pltpu.VMEM((1, 128), F32)],
        compiler_params=pltpu.CompilerParams(dimension_semantics=("arbitrary",), vmem_limit_bytes=VMEM_LIMIT),
        name="mix_out",
    )(y, r, k, v, g, ga, part, x, *consts)


def _experts_body(be_ref, nb_ref, x_ref, wg_ref, wu_ref, wd_ref, o_ref, wg_s, wu_s, wd_s):
    i = pl.program_id(0)
    used = i < nb_ref[0]

    @pl.when(used & ((i == 0) | (be_ref[i] != be_ref[jnp.maximum(i - 1, 0)])))
    def _():
        wg_s[...] = _bf(wg_ref[...])
        wu_s[...] = _bf(wu_ref[...])
        wd_s[...] = _bf(wd_ref[...])

    @pl.when(used)
    def _():
        hb = EXPERT_BLOCK // 2
        rows = [pl.ds(h * hb, hb) for h in range(2)]
        xs = [x_ref[rs, :] for rs in rows]
        hg = [_dot(a, wg_s[...]) for a in xs]
        hu = [_dot(a, wu_s[...]) for a in xs]
        hh = [_bf(a * _sigmoid(a) * b) for a, b in zip(hg, hu)]
        out = [_dot(a, wd_s[...]) for a in hh]
        for rs, a in zip(rows, out):
            o_ref[rs, :] = _bf(a)

    @pl.when(jnp.logical_not(used))
    def _():
        o_ref[...] = jnp.zeros(o_ref.shape, BF16)


def _experts(xg, block_expert, n_used, wg, wu, wd, layer):
    rows = xg.shape[0]
    nb = rows // EXPERT_BLOCK
    grid_spec = pltpu.PrefetchScalarGridSpec(
        num_scalar_prefetch=2,
        grid=(nb,),
        in_specs=[
            pl.BlockSpec((EXPERT_BLOCK, D_MODEL), lambda i, be, nu: (i, 0)),
            pl.BlockSpec((None, None, D_MODEL, D_EXPERT), lambda i, be, nu: (layer, be[i], 0, 0)),
            pl.BlockSpec((None, None, D_MODEL, D_EXPERT), lambda i, be, nu: (layer, be[i], 0, 0)),
            pl.BlockSpec((None, None, D_EXPERT, D_MODEL), lambda i, be, nu: (layer, be[i], 0, 0)),
        ],
        out_specs=pl.BlockSpec((EXPERT_BLOCK, D_MODEL), lambda i, be, nu: (i, 0)),
        scratch_shapes=[pltpu.VMEM((D_MODEL, D_EXPERT), BF16), pltpu.VMEM((D_MODEL, D_EXPERT), BF16),
                        pltpu.VMEM((D_EXPERT, D_MODEL), BF16)],
    )
    return pl.pallas_call(
        _experts_body,
        out_shape=jax.ShapeDtypeStruct((rows, D_MODEL), BF16),
        grid_spec=grid_spec,
        compiler_params=pltpu.CompilerParams(dimension_semantics=("arbitrary",), vmem_limit_bytes=VMEM_LIMIT),
        name="experts",
    )(block_expert, n_used, xg, wg, wu, wd)


def _combine_body(x_ref, y0_ref, y1_ref, route_ref, g_ref, b_ref, o_ref):
    route = route_ref[...]
    moe = route[:, 2:3] * y0_ref[...] + route[:, 3:4] * y1_ref[...]
    o_ref[...] = _layer_norm(ALPHA * x_ref[...] + moe, g_ref[...], b_ref[...])


def _combine(x1, y0, y1, route, ln_g, ln_b):
    t = x1.shape[0]
    ts = SEQ_TILE
    row = lambda c: pl.BlockSpec((ts, c), lambda i: (i, 0))
    return pl.pallas_call(
        _combine_body,
        out_shape=jax.ShapeDtypeStruct((t, D_MODEL), F32),
        grid=(t // ts,),
        in_specs=[row(D_MODEL)] * 3 + [row(128), _const_spec(ln_g.shape), _const_spec(ln_b.shape)],
        out_specs=row(D_MODEL),
        compiler_params=pltpu.CompilerParams(dimension_semantics=("arbitrary",), vmem_limit_bytes=VMEM_LIMIT),
        name="combine",
    )(x1, y0, y1, route, ln_g, ln_b)


def _dispatch(route, counts, t):
    expert = route[:, 0:2].astype(jnp.int32)
    rank = route[:, 4:6].astype(jnp.int32)
    counts = counts[0, :N_EXPERTS].astype(jnp.int32)
    padded = (counts + EXPERT_BLOCK - 1) // EXPERT_BLOCK * EXPERT_BLOCK
    pad_end = jnp.cumsum(padded)
    pad_start = pad_end - padded
    onehot = expert[:, :, None] == jnp.arange(N_EXPERTS, dtype=jnp.int32)[None, None, :]
    dest = jnp.sum(jnp.where(onehot, pad_start[None, None, :], 0), axis=-1) + rank
    n_blocks = (2 * t + EXPERT_BLOCK - 1) // EXPERT_BLOCK + N_EXPERTS
    block_start = jnp.arange(n_blocks, dtype=jnp.int32) * EXPERT_BLOCK
    block_expert = jnp.minimum(jnp.sum((pad_end[None, :] <= block_start[:, None]).astype(jnp.int32), axis=1),
                               N_EXPERTS - 1)
    n_used = (pad_end[-1:] // EXPERT_BLOCK).astype(jnp.int32)
    return dest, block_expert, n_used, n_blocks * EXPERT_BLOCK


def _pad_cols(w, width=LORA_PAD):
    return jnp.pad(w, ((0, 0), (0, width - w.shape[1])))


def _pad_rows(w, height=LORA_PAD):
    return jnp.pad(w, ((0, height - w.shape[0]), (0, 0)))


def _row(v):
    return v.reshape(1, -1).astype(F32)


def _layer_params(l, a):
    w_in = a['w_in_first'] if l == 0 else a['w_in_rest'][l - 1]
    if l == 0:
        w_vres = jnp.zeros((D_MODEL, LORA_PAD), F32)
        mu_vres = jnp.zeros((LORA_PAD,), F32)
    else:
        w_vres = _pad_cols(w_in[:, _O_VRES:])
        mu_vres = jnp.pad(a['rwkv_mu_vres'][l - 1], (0, LORA_PAD - 32))
    mu = a['rwkv_mu'][l]
    p = {
        'w_rw': _bf(jnp.concatenate([w_in[:, :_O_WL], _pad_cols(w_in[:, _O_WL:_O_AL]), _pad_cols(w_in[:, _O_AL:_O_GL]),
                                     _pad_cols(w_in[:, _O_GL:_O_SGU]), w_vres], axis=1)),
        'w_sgu': _bf(w_in[:, _O_SGU:_O_CONV]),
        'w_conv': _bf(w_in[:, _O_CONV:_O_GATE]),
        'w_gate': _bf(w_in[:, _O_GATE:_O_VRES]),
        'mu': _row(jnp.concatenate([mu[:_O_WL], jnp.pad(mu[_O_WL:_O_AL], (0, 96)), jnp.pad(mu[_O_AL:_O_GL], (0, 96)),
                                    jnp.pad(mu[_O_GL:], (0, 32)), mu_vres])),
        'w0': _row(a['rwkv_w0'][l]),
        'w_up': _bf(_pad_rows(a['rwkv_w_up'][l])),
        'a0': _row(a['rwkv_a0'][l]),
        'a_up': _bf(_pad_rows(a['rwkv_a_up'][l])),
        'g_up': _bf(_pad_rows(a['rwkv_g_up'][l])),
        'k_k': _row(a['rwkv_k_k'][l]),
        'k_a': _row(a['rwkv_k_a'][l]),
        'r_k': _row(a['rwkv_r_k'][l]),
        'gn_g': _row(a['rwkv_gn_g'][l]),
        'gn_b': _row(a['rwkv_gn_b'][l]),
        'rwkv_w_o': _bf(a['rwkv_w_o'][l]),
        'sgu_ln_g': _row(a['sgu_ln_g'][l]),
        'sgu_ln_b': _row(a['sgu_ln_b'][l]),
        'sgu_w': jnp.transpose(a['sgu_w'][l], (1, 0, 2)).reshape(SGU_CHUNK, SGU_GROUPS * SGU_CHUNK),
        'sgu_bias': jnp.repeat(jnp.transpose(a['sgu_b'][l]), SGU_W // SGU_GROUPS, axis=1),
        'sgu_w_o': _bf(a['sgu_w_o'][l]),
        'conv_dw': jnp.pad(a['conv_dw'][l], ((0, 1), (0, 0))),
        'conv_db': _row(a['conv_db'][l]),
        'conv_ln_g': _row(a['conv_ln_g'][l]),
        'conv_ln_b': _row(a['conv_ln_b'][l]),
        'conv_w_o': _bf(a['conv_w_o'][l]),
        'w_out': _bf(a['w_out'][l]),
        'ln1_g': _row(a['ln1_g'][l]),
        'ln1_b': _row(a['ln1_b'][l]),
        'ln2_g': _row(a['ln2_g'][l]),
        'ln2_b': _row(a['ln2_b'][l]),
    }
    if l > 0:
        p['v0'] = _row(a['rwkv_v0'][l - 1])
        p['v_up'] = _bf(_pad_rows(a['rwkv_v_up'][l - 1]))
    return p


def _forward(a):
    x = a['x']
    bsz, seq, _ = x.shape
    t = bsz * seq
    x = x.reshape(t, D_MODEL)
    head = jnp.arange(RW, dtype=jnp.int32) // HEAD_DIM
    head_sum = _bf((head[:, None] == head[None, :]).astype(F32))
    rw = _pad_cols(a['router_w'])
    rw_hi = _bf(rw)
    rw_lo = _bf(rw - rw_hi.astype(F32))
    rb = _row(jnp.pad(a['router_b'], (0, LORA_PAD - N_EXPERTS)))
    v_first = None
    for l in range(DEPTH):
        p = _layer_params(l, a)
        p.update(head_sum=head_sum, router_hi=rw_hi, router_lo=rw_lo, router_b=rb)
        r, lw, k, v, na, nb, g, ga, part = _mix_in(x, p, v_first, bsz, seq)
        if l == 0:
            v_first = v
        y = _wkv(r, lw, k, v, na, nb, bsz, seq)
        x1, x1b, route, counts = _mix_out(y, r, k, v, g, ga, part, x, p, bsz, seq)
        dest, block_expert, n_used, rows = _dispatch(route, counts, t)
        xg = jnp.zeros((rows, D_MODEL), BF16)
        xg = xg.at[dest[:, 0]].set(x1b, unique_indices=True).at[dest[:, 1]].set(x1b, unique_indices=True)
        yb = _experts(xg, block_expert, n_used, a['moe_w_gate'], a['moe_w_up'], a['moe_w_down'], l)
        y0 = jnp.take(yb, dest[:, 0], axis=0)
        y1 = jnp.take(yb, dest[:, 1], axis=0)
        x = _combine(x1, y0, y1, route, p['ln2_g'], p['ln2_b'])
    return x.reshape(bsz, seq, D_MODEL)


def kernel(x, w_in_first, w_in_rest, rwkv_mu, rwkv_mu_vres, rwkv_w0, rwkv_w_up, rwkv_a0, rwkv_a_up, rwkv_g_up, rwkv_v0, rwkv_v_up, rwkv_k_k, rwkv_k_a, rwkv_r_k, rwkv_gn_g, rwkv_gn_b, rwkv_w_o, sgu_ln_g, sgu_ln_b, sgu_w, sgu_b, sgu_w_o, conv_dw, conv_db, conv_ln_g, conv_ln_b, conv_w_o, w_out, ln1_g, ln1_b, router_w, router_b, moe_w_gate, moe_w_up, moe_w_down, ln2_g, ln2_b):
    return _forward(dict(
        x=x, w_in_first=w_in_first, w_in_rest=w_in_rest, rwkv_mu=rwkv_mu, rwkv_mu_vres=rwkv_mu_vres,
        rwkv_w0=rwkv_w0, rwkv_w_up=rwkv_w_up, rwkv_a0=rwkv_a0, rwkv_a_up=rwkv_a_up, rwkv_g_up=rwkv_g_up,
        rwkv_v0=rwkv_v0, rwkv_v_up=rwkv_v_up, rwkv_k_k=rwkv_k_k, rwkv_k_a=rwkv_k_a, rwkv_r_k=rwkv_r_k,
        rwkv_gn_g=rwkv_gn_g, rwkv_gn_b=rwkv_gn_b, rwkv_w_o=rwkv_w_o, sgu_ln_g=sgu_ln_g, sgu_ln_b=sgu_ln_b,
        sgu_w=sgu_w, sgu_b=sgu_b, sgu_w_o=sgu_w_o, conv_dw=conv_dw, conv_db=conv_db, conv_ln_g=conv_ln_g,
        conv_ln_b=conv_ln_b, conv_w_o=conv_w_o, w_out=w_out, ln1_g=ln1_g, ln1_b=ln1_b, router_w=router_w,
        router_b=router_b, moe_w_gate=moe_w_gate, moe_w_up=moe_w_up, moe_w_down=moe_w_down, ln2_g=ln2_g,
        ln2_b=ln2_b))
```

```python
import functools

import jax
import jax.numpy as jnp
from jax import lax
from jax.experimental import pallas as pl
from jax.experimental.pallas import tpu as pltpu

F32 = jnp.float32
BF16 = jnp.bfloat16

D_MODEL = 1024
DEPTH = 4
HEADS = 8
HEAD_DIM = 64
RW = HEADS * HEAD_DIM
LORA_PAD = 128
RW_COLS = 3 * RW + 4 * LORA_PAD
SGU_W = 256
SGU_CHUNK = 128
SGU_GROUPS = 4
CONV_W = 256
CONV_K = 31
N_EXPERTS = 32
GROUP_SIZE = 8
N_GROUPS = 4
D_EXPERT = 512
EXPERT_BLOCK = 256
ALPHA = (2 * DEPTH) ** 0.25
LN_EPS = 1e-5
GN_EPS = 64e-5

_O_WL = 3 * RW
_O_AL = _O_WL + 32
_O_GL = _O_AL + 32
_O_SGU = _O_GL + 96
_O_CONV = _O_SGU + 2 * SGU_W
_O_GATE = _O_CONV + 2 * CONV_W
_O_VRES = _O_GATE + 3 * D_MODEL

SEQ_TILE = 256
WKV_CHUNK = 64
WKV_TILE = 256
MIX_OUT_HALVES = 2
PAIR = 2 * HEAD_DIM
HEAD_SLAB = 256
VMEM_LIMIT = 48 * 1024 * 1024


def _dot(a, b):
    return jnp.dot(a, b, preferred_element_type=F32)


def _dot_nt(a, b):
    return lax.dot_general(a, b, (((1,), (1,)), ((), ())), preferred_element_type=F32)


def _bf(x):
    return x.astype(BF16)


def _head_sum(x_bf16, ones_bd):
    w = ones_bd.shape[0]
    return jnp.concatenate([_dot(x_bf16[:, i:i + w], ones_bd) for i in range(0, x_bf16.shape[1], w)], axis=1)


def _head_sum_split(x, ones_bd):
    hi = _bf(x)
    lo = _bf(x - hi.astype(F32))
    return _head_sum(hi, ones_bd) + _head_sum(lo, ones_bd)


def _split_dot_left(w_bf16, x):
    hi = _bf(x)
    lo = _bf(x - hi.astype(F32))
    return _dot(w_bf16, hi) + _dot(w_bf16, lo)


def _layer_norm(x, g, b):
    mu = jnp.mean(x, axis=-1, keepdims=True)
    d = x - mu
    var = jnp.mean(d * d, axis=-1, keepdims=True)
    return d * lax.rsqrt(var + LN_EPS) * g + b


def _sigmoid(x):
    return 1.0 / (1.0 + jnp.exp(-x))


def _const_spec(shape):
    nd = len(shape)
    return pl.BlockSpec(shape, lambda *_: (0,) * nd, pipeline_mode=pl.Buffered(1))


def _mix_in_body(has_vres, ts, *refs):
    it = iter(refs)
    x_ref, wrw_ref, wsgu_ref, wconv_ref, wgate_ref = (next(it) for _ in range(5))
    mu_ref, w0_ref, wup_ref, a0_ref, aup_ref, gup_ref, kk_ref, ka_ref = (next(it) for _ in range(8))
    if has_vres:
        v0_ref, vup_ref, vfirst_ref = (next(it) for _ in range(3))
    hs_ref = next(it)
    slng_ref, slnb_ref, sw_ref, sbias_ref, swo_ref = (next(it) for _ in range(5))
    cdw_ref, cdb_ref, clng_ref, clnb_ref, cwo_ref = (next(it) for _ in range(5))
    r_o, lw_o, k_o, v_o, a_o, b_o, g_o, ga_o, part_o = (next(it) for _ in range(9))
    pbuf, hext, hshift = next(it), next(it), next(it)

    @pl.when(pl.program_id(1) == 0)
    def _():
        pbuf[0:8, :] = jnp.zeros((8, RW_COLS), F32)
        hext[0:32, :] = jnp.zeros((32, CONV_W), F32)

    xb = _bf(x_ref[...])
    pr = _dot(xb, wrw_ref[...])
    pb = _dot(xb, wsgu_ref[...])
    pc = _dot(xb, wconv_ref[...])
    gate_logits = _dot(xb, wgate_ref[...])

    pbuf[8:8 + ts, :] = pr
    prev = pbuf[pl.ds(7, ts), :]
    pa = pr + (prev - pr) * mu_ref[...]
    pbuf[0:8, :] = pbuf[ts:ts + 8, :]
    r = pa[:, 0:RW]
    k = pa[:, RW:2 * RW]
    v = pa[:, 2 * RW:3 * RW]
    wl = pa[:, 3 * RW:3 * RW + LORA_PAD]
    al = pa[:, 3 * RW + LORA_PAD:3 * RW + 2 * LORA_PAD]
    gl = pa[:, 3 * RW + 2 * LORA_PAD:3 * RW + 3 * LORA_PAD]
    z = w0_ref[...] + _dot(_bf(jnp.tanh(wl)), wup_ref[...])
    nz = -z
    softplus = jnp.maximum(nz, 0.0) + jnp.log(1.0 + jnp.exp(-jnp.abs(nz)))
    lw_o[...] = -jnp.exp(-softplus - 0.5)
    iclr = _sigmoid(a0_ref[...] + _dot(_bf(al), aup_ref[...]))
    g_o[...] = _dot(_bf(_sigmoid(gl)), gup_ref[...])
    kkv = k * kk_ref[...]
    ss = _head_sum(_bf(kkv * kkv), hs_ref[...])
    kkn = kkv / jnp.maximum(jnp.sqrt(ss), 1e-12)
    k_o[...] = k * (1.0 + (iclr - 1.0) * ka_ref[...])
    if has_vres:
        vl = pa[:, 3 * RW + 3 * LORA_PAD:RW_COLS]
        v = v + (vfirst_ref[...] - v) * _sigmoid(v0_ref[...] + _dot(_bf(vl), vup_ref[...]))
    r_o[...] = r
    v_o[...] = v
    a_o[...] = -kkn
    b_o[...] = kkn * iclr

    zz = 0.5 * pb * (1.0 + lax.erf(pb * (2.0 ** -0.5)))
    u = zz[:, :SGU_W]
    vv = _layer_norm(zz[:, SGU_W:], slng_ref[...], slnb_ref[...])
    prow = lax.broadcasted_iota(jnp.int32, (SGU_CHUNK, SGU_GROUPS * SGU_CHUNK), 0)
    qcol = lax.broadcasted_iota(jnp.int32, (SGU_CHUNK, SGU_GROUPS * SGU_CHUNK), 1) % SGU_CHUNK
    wc = _bf(jnp.where(qcol <= prow, sw_ref[...], 0.0))
    lane_grp = lax.broadcasted_iota(jnp.int32, (SGU_CHUNK, SGU_W), 1) // (SGU_W // SGU_GROUPS)
    mixed = []
    for c in range(ts // SGU_CHUNK):
        vc = vv[c * SGU_CHUNK:(c + 1) * SGU_CHUNK]
        stack = jnp.concatenate([jnp.where(lane_grp == g, vc, 0.0) for g in range(SGU_GROUPS)], axis=0)
        mixed.append(_dot(wc, _bf(stack)) + sbias_ref[...])
    mixed = jnp.concatenate(mixed, axis=0)
    o_b = _dot(_bf(u * mixed), swo_ref[...])

    hext[32:32 + ts, :] = pc[:, :CONV_W] * _sigmoid(pc[:, CONV_W:])
    acc = jnp.zeros((ts, CONV_W), F32) + cdb_ref[...]
    first = 32 - (CONV_K - 1)
    for rem in range(8):
        taps = [j for j in range(CONV_K) if (first + j) % 8 == rem]
        span = max(first + j for j in taps) - rem
        hshift[0:span + ts, :] = hext[pl.ds(rem, span + ts), :]
        for j in taps:
            off = first + j - rem
            acc = acc + cdw_ref[j:j + 1, :] * hshift[off:off + ts, :]
    hext[0:32, :] = hext[ts:ts + 32, :]
    hc = _layer_norm(acc, clng_ref[...], clnb_ref[...])
    hc = hc * _sigmoid(hc)
    o_c = _dot(_bf(hc), cwo_ref[...])

    gates = _sigmoid(_bf(gate_logits))
    ga_o[...] = gates[:, :D_MODEL]
    part_o[...] = _bf(gates[:, D_MODEL:2 * D_MODEL] * o_b + gates[:, 2 * D_MODEL:] * o_c)


def _mix_in(x, p, v_first, bsz, seq):
    t = bsz * seq
    ts = SEQ_TILE
    nst = seq // ts
    has_vres = v_first is not None
    row = lambda c: pl.BlockSpec((ts, c), lambda b, s: (b * nst + s, 0))
    args = [x, p['w_rw'], p['w_sgu'], p['w_conv'], p['w_gate'], p['mu'], p['w0'], p['w_up'], p['a0'], p['a_up'],
            p['g_up'], p['k_k'], p['k_a']]
    specs = [row(D_MODEL)] + [_const_spec(a.shape) for a in args[1:]]
    if has_vres:
        args += [p['v0'], p['v_up'], v_first]
        specs += [_const_spec(p['v0'].shape), _const_spec(p['v_up'].shape), row(RW)]
    tail = [p['head_sum'], p['sgu_ln_g'], p['sgu_ln_b'], p['sgu_w'], p['sgu_bias'], p['sgu_w_o'],
            p['conv_dw'], p['conv_db'], p['conv_ln_g'], p['conv_ln_b'], p['conv_w_o']]
    args += tail
    specs += [_const_spec(a.shape) for a in tail]
    out_shape = [jax.ShapeDtypeStruct((t, RW), F32)] * 7 + [jax.ShapeDtypeStruct((t, D_MODEL), BF16)] * 2
    out_specs = [row(RW)] * 7 + [row(D_MODEL)] * 2
    return pl.pallas_call(
        functools.partial(_mix_in_body, has_vres, ts),
        out_shape=out_shape,
        grid=(bsz, nst),
        in_specs=specs,
        out_specs=out_specs,
        scratch_shapes=[pltpu.VMEM((ts + 8, RW_COLS), F32), pltpu.VMEM((ts + 32, CONV_W), F32),
                        pltpu.VMEM((ts + 32, CONV_W), F32)],
        compiler_params=pltpu.CompilerParams(dimension_semantics=("arbitrary", "arbitrary"),
                                             vmem_limit_bytes=VMEM_LIMIT),
        name="mix_in",
    )(*args)


def _wkv_body(tb, r_ref, lw_ref, k_ref, v_ref, a_ref, b_ref, y_ref, st_ref, wr_s, rs_s, ar_s, vs_s, kb_s, dc_s):
    c = WKV_CHUNK
    nc = tb // c
    npair = HEADS // 2
    units = [(ci, p) for ci in range(nc) for p in range(npair)]

    @pl.when(pl.program_id(1) == 0)
    def _():
        st_ref[...] = jnp.zeros(st_ref.shape, F32)

    lane = lax.broadcasted_iota(jnp.int32, (c, PAIR), 1)
    head0 = lane < HEAD_DIM
    ti = lax.broadcasted_iota(jnp.int32, (PAIR, PAIR), 0)
    si = lax.broadcasted_iota(jnp.int32, (PAIR, PAIR), 1)
    same = (ti // c) == (si // c)
    lower = same & ((si % c) <= (ti % c))
    strict = same & ((si % c) < (ti % c))
    eye = (ti == si).astype(F32)
    tr = lax.broadcasted_iota(jnp.int32, (tb, tb), 0)
    tc = lax.broadcasted_iota(jnp.int32, (tb, tb), 1)
    tri = _bf(((tr // c == tc // c) & (tc <= tr)).astype(F32))

    def stack(x):
        return jnp.concatenate([jnp.where(head0, x, 0.0), jnp.where(head0, 0.0, x)], axis=0)

    lw = lw_ref[...]
    cum = _split_dot_left(tri, lw)
    cum_end = jnp.concatenate(
        [jnp.broadcast_to(cum[(ci + 1) * c - 1:(ci + 1) * c, :], (c, RW)) for ci in range(nc)], axis=0)
    p_inv = jnp.exp(-cum)
    p_end = jnp.exp(cum_end - cum)
    rt = r_ref[...] * jnp.exp(cum)
    at = a_ref[...] * jnp.exp(cum - lw)
    kt = k_ref[...] * p_inv
    bt = b_ref[...] * p_inv
    ke = k_ref[...] * p_end
    be = b_ref[...] * p_end
    vt = v_ref[...]

    def sub(x, ci, p):
        return x[ci * c:(ci + 1) * c, p * PAIR:(p + 1) * PAIR]

    a_s = [stack(sub(at, ci, p)) for ci, p in units]
    r_s = [stack(sub(rt, ci, p)) for ci, p in units]
    g = [_dot_nt(_bf(jnp.concatenate([r_s[i], a_s[i]], axis=0)),
                 _bf(jnp.concatenate([stack(sub(kt, ci, p)), stack(sub(bt, ci, p))], axis=0)))
         for i, (ci, p) in enumerate(units)]
    for i, (ci, p) in enumerate(units):
        vs_s[ci, p] = _bf(stack(sub(vt, ci, p)))
        ar_s[ci, p] = _bf(jnp.concatenate([jnp.where(lower, g[i][0:PAIR, 0:PAIR], 0.0),
                                           jnp.where(lower, g[i][0:PAIR, PAIR:], 0.0)], axis=1))
        kb_s[ci, p] = _bf(jnp.concatenate([stack(sub(ke, ci, p)).T, stack(sub(be, ci, p)).T], axis=1))
        dc_s[ci, p] = jnp.exp(jnp.broadcast_to(sub(cum_end, ci, p)[0:1, :], (PAIR, PAIR))).T
    apow = [jnp.where(strict, g[i][PAIR:, PAIR:], 0.0) for i in range(len(units))]
    inv = [eye + x for x in apow]
    apb = [_bf(x) for x in apow]
    apow = [_dot(x, x) for x in apb]
    for _ in range(4):
        apb = [_bf(x) for x in apow]
        both = [_dot(x, jnp.concatenate([x, _bf(y)], axis=1)) for x, y in zip(apb, inv)]
        apow = [x[:, :PAIR] for x in both]
        inv = [y + x[:, PAIR:] for x, y in zip(both, inv)]
    inv = [y + _dot(_bf(x), _bf(y)) for x, y in zip(apow, inv)]
    for i, (ci, p) in enumerate(units):
        a_ak = jnp.where(strict, g[i][PAIR:, 0:PAIR], 0.0)
        wr_s[ci, p] = _bf(_dot(_bf(inv[i]), _bf(jnp.concatenate([a_s[i], a_ak], axis=1))))
        rs_s[ci, p] = _bf(r_s[i])

    s = [st_ref[p] for p in range(npair)]
    for ci in range(nc):
        sb = [_bf(x) for x in s]
        u = [_dot(wr_s[ci, p], jnp.concatenate([sb[p], vs_s[ci, p]], axis=0)) for p in range(npair)]
        vu = [jnp.concatenate([vs_s[ci, p], _bf(u[p])], axis=0) for p in range(npair)]
        s = [dc_s[ci, p] * s[p] + _dot(kb_s[ci, p], vu[p]) for p in range(npair)]
        for p in range(npair):
            y = _dot(rs_s[ci, p], sb[p]) + _dot(ar_s[ci, p], vu[p])
            y_ref[ci * c:(ci + 1) * c, p * PAIR:(p + 1) * PAIR] = y[0:c] + y[c:]
    for p in range(npair):
        st_ref[p] = s[p]


def _wkv(r, lw, k, v, a, b, bsz, seq):
    t = bsz * seq
    tb = WKV_TILE
    nst = seq // tb
    nc, npair = tb // WKV_CHUNK, HEADS // 2
    row = pl.BlockSpec((tb, RW), lambda bi, s: (bi * nst + s, 0))
    return pl.pallas_call(
        functools.partial(_wkv_body, tb),
        out_shape=jax.ShapeDtypeStruct((t, RW), F32),
        grid=(bsz, nst),
        in_specs=[row] * 6,
        out_specs=row,
        scratch_shapes=[pltpu.VMEM((npair, PAIR, PAIR), F32),
                        pltpu.VMEM((nc, npair, PAIR, 2 * PAIR), BF16),
                        pltpu.VMEM((nc, npair, PAIR, PAIR), BF16),
                        pltpu.VMEM((nc, npair, PAIR, 2 * PAIR), BF16),
                        pltpu.VMEM((nc, npair, PAIR, PAIR), BF16),
                        pltpu.VMEM((nc, npair, PAIR, 2 * PAIR), BF16),
                        pltpu.VMEM((nc, npair, PAIR, PAIR), F32)],
        compiler_params=pltpu.CompilerParams(dimension_semantics=("arbitrary", "arbitrary"),
                                             vmem_limit_bytes=VMEM_LIMIT),
        name="wkv",
    )(r, lw, k, v, a, b)


def _mix_out_body(halves, y_ref, r_ref, k_ref, v_ref, g_ref, ga_ref, part_ref, x_ref, hs_ref, rk_ref, gng_ref,
                  gnb_ref, wo_ref, wout_ref, ln_g_ref, ln_b_ref, rw_hi_ref, rw_cat_ref, rb_ref, x1_o, x1b_o, route_o,
                  counts_o, cnt_ref):
    th = y_ref.shape[0] // halves
    hrows = [pl.ds(h * th, th) for h in range(halves)]

    def each(fn, *cols):
        return [fn(*a) for a in zip(*cols)]

    def load(ref):
        return [ref[rs, :] for rs in hrows]

    hs = hs_ref[...]
    inv_n = 1.0 / HEAD_DIM
    y = load(y_ref)
    mu = each(lambda a: _head_sum_split(a, hs) * inv_n, y)
    d = each(lambda a, m: a - m, y, mu)
    var = each(lambda a: _head_sum(_bf(a * a), hs) * inv_n, d)
    rkk = each(lambda a, b: _head_sum(_bf(a * b * rk_ref[...]), hs), load(r_ref), load(k_ref))
    yn = each(lambda a, s2: a * lax.rsqrt(s2 + GN_EPS) * gng_ref[...] + gnb_ref[...], d, var)
    out = each(lambda a, bo, vv, gg: _bf((a + bo * vv) * gg), yn, rkk, load(v_ref), load(g_ref))
    o_a = each(lambda a: _dot(a, wo_ref[...]), out)
    merged = each(lambda a, b, c2: _bf(a * b + c2), load(ga_ref), o_a, load(part_ref))
    xm = each(lambda a: _dot(a, wout_ref[...]), merged)
    x1 = each(lambda a, b: _layer_norm(ALPHA * a + b, ln_g_ref[...], ln_b_ref[...]), load(x_ref), xm)
    xh = each(_bf, x1)
    for rs, a, b in zip(hrows, x1, xh):
        x1_o[rs, :] = a
        x1b_o[rs, :] = b

    xl = each(lambda a, b: _bf(a - b.astype(F32)), x1, xh)
    hcat = each(lambda a: _dot(a, rw_cat_ref[...]), xh)
    logits = each(lambda c2, b: c2[:, :128] + c2[:, 128:] + _dot(b, rw_hi_ref[...]), hcat, xl)
    scores = each(_sigmoid, logits)
    sel = each(lambda a: a + rb_ref[...], scores)
    lane = lax.broadcasted_iota(jnp.int32, (th, 128), 1).astype(F32)
    neg = jnp.float32(-jnp.inf)

    def top2(sg):
        m1 = jnp.max(sg, axis=-1, keepdims=True)
        i1 = jnp.min(jnp.where(sg == m1, lane, 1e9), axis=-1, keepdims=True)
        sg2 = jnp.where(lane == i1, neg, sg)
        m2 = jnp.max(sg2, axis=-1, keepdims=True)
        i2 = jnp.min(jnp.where(sg2 == m2, lane, 1e9), axis=-1, keepdims=True)
        return m1 + m2, i1, i2

    best = e0 = e1 = None
    for g in range(N_GROUPS):
        in_g = (lane >= g * GROUP_SIZE) & (lane < (g + 1) * GROUP_SIZE)
        cand = each(lambda a: top2(jnp.where(in_g, a, neg)), sel)
        if g == 0:
            best, e0, e1 = ([c[i] for c in cand] for i in range(3))
        else:
            take = each(lambda c, b: c[0] > b, cand, best)
            best = each(lambda tk, c, b: jnp.where(tk, c[0], b), take, cand, best)
            e0 = each(lambda tk, c, b: jnp.where(tk, c[1], b), take, cand, e0)
            e1 = each(lambda tk, c, b: jnp.where(tk, c[2], b), take, cand, e1)
    g0 = each(lambda e, sc: jnp.sum(jnp.where(lane == e, sc, 0.0), axis=-1, keepdims=True), e0, scores)
    g1 = each(lambda e, sc: jnp.sum(jnp.where(lane == e, sc, 0.0), axis=-1, keepdims=True), e1, scores)

    @pl.when(pl.program_id(0) == 0)
    def _():
        cnt_ref[...] = jnp.zeros(cnt_ref.shape, F32)

    chosen = each(lambda a, b: jnp.where((lane == a) | (lane == b), 1.0, 0.0), e0, e1)
    before = _bf((lax.broadcasted_iota(jnp.int32, (th, th), 1)
                  < lax.broadcasted_iota(jnp.int32, (th, th), 0)).astype(F32))
    inside = each(lambda a: _dot(before, _bf(a)), chosen)
    total = each(lambda a: jnp.sum(a, axis=0, keepdims=True), chosen)
    base = cnt_ref[...]
    for h in range(halves):
        seen = inside[h] + base
        rank0 = jnp.sum(jnp.where(lane == e0[h], seen, 0.0), axis=-1, keepdims=True)
        rank1 = jnp.sum(jnp.where(lane == e1[h], seen, 0.0), axis=-1, keepdims=True)
        tot = g0[h] + g1[h]
        route_o[hrows[h], :] = jnp.where(
            lane == 0, e0[h], jnp.where(lane == 1, e1[h], jnp.where(lane == 2, g0[h] / tot, jnp.where(
                lane == 3, g1[h] / tot, jnp.where(lane == 4, rank0, jnp.where(lane == 5, rank1, 0.0))))))
        base = base + total[h]
    cnt_ref[...] = base
    counts_o[...] = base


def _mix_out(y, r, k, v, g, ga, part, x, p, bsz, seq):
    t = bsz * seq
    ts = MIX_OUT_HALVES * SEQ_TILE
    row = lambda c: pl.BlockSpec((ts, c), lambda i: (i, 0))
    consts = [p['head_sum'], p['r_k'], p['gn_g'], p['gn_b'], p['rwkv_w_o'], p['w_out'], p['ln1_g'], p['ln1_b'],
              p['router_hi'], p['router_cat'], p['router_b']]
    return pl.pallas_call(
        functools.partial(_mix_out_body, MIX_OUT_HALVES),
        out_shape=[jax.ShapeDtypeStruct((t, D_MODEL), F32), jax.ShapeDtypeStruct((t, D_MODEL), BF16),
                   jax.ShapeDtypeStruct((t, 128), F32), jax.ShapeDtypeStruct((1, 128), F32)],
        grid=(t // ts,),
        in_specs=[row(RW)] * 5 + [row(D_MODEL)] * 3 + [_const_spec(a.shape) for a in consts],
        out_specs=[row(D_MODEL), row(D_MODEL), row(128), pl.BlockSpec((1, 128), lambda i: (0, 0))],
        scratch_shapes=[pltpu.VMEM((1, 128), F32)],
        compiler_params=pltpu.CompilerParams(dimension_semantics=("arbitrary",), vmem_limit_bytes=VMEM_LIMIT),
        name="mix_out",
    )(y, r, k, v, g, ga, part, x, *consts)


def _experts_body(be_ref, nb_ref, x_ref, wg_ref, wu_ref, wd_ref, o_ref, wg_s, wu_s, wd_s):
    i = pl.program_id(0)
    used = i < nb_ref[0]

    @pl.when(used & ((i == 0) | (be_ref[i] != be_ref[jnp.maximum(i - 1, 0)])))
    def _():
        wg_s[...] = _bf(wg_ref[...])
        wu_s[...] = _bf(wu_ref[...])
        wd_s[...] = _bf(wd_ref[...])

    @pl.when(used)
    def _():
        hb = EXPERT_BLOCK // 2
        rows = [pl.ds(h * hb, hb) for h in range(2)]
        xs = [x_ref[rs, :] for rs in rows]
        hg = [_dot(a, wg_s[...]) for a in xs]
        hu = [_dot(a, wu_s[...]) for a in xs]
        hh = [_bf(a * _sigmoid(a) * b) for a, b in zip(hg, hu)]
        out = [_dot(a, wd_s[...]) for a in hh]
        for rs, a in zip(rows, out):
            o_ref[rs, :] = _bf(a)

    @pl.when(jnp.logical_not(used))
    def _():
        o_ref[...] = jnp.zeros(o_ref.shape, BF16)


def _experts(xg, block_expert, n_used, wg, wu, wd, layer):
    rows = xg.shape[0]
    nb = rows // EXPERT_BLOCK
    grid_spec = pltpu.PrefetchScalarGridSpec(
        num_scalar_prefetch=2,
        grid=(nb,),
        in_specs=[
            pl.BlockSpec((EXPERT_BLOCK, D_MODEL), lambda i, be, nu: (i, 0)),
            pl.BlockSpec((None, None, D_MODEL, D_EXPERT), lambda i, be, nu: (layer, be[i], 0, 0)),
            pl.BlockSpec((None, None, D_MODEL, D_EXPERT), lambda i, be, nu: (layer, be[i], 0, 0)),
            pl.BlockSpec((None, None, D_EXPERT, D_MODEL), lambda i, be, nu: (layer, be[i], 0, 0)),
        ],
        out_specs=pl.BlockSpec((EXPERT_BLOCK, D_MODEL), lambda i, be, nu: (i, 0)),
        scratch_shapes=[pltpu.VMEM((D_MODEL, D_EXPERT), BF16), pltpu.VMEM((D_MODEL, D_EXPERT), BF16),
                        pltpu.VMEM((D_EXPERT, D_MODEL), BF16)],
    )
    return pl.pallas_call(
        _experts_body,
        out_shape=jax.ShapeDtypeStruct((rows, D_MODEL), BF16),
        grid_spec=grid_spec,
        compiler_params=pltpu.CompilerParams(dimension_semantics=("arbitrary",), vmem_limit_bytes=VMEM_LIMIT),
        name="experts",
    )(block_expert, n_used, xg, wg, wu, wd)


def _combine_body(x_ref, y0_ref, y1_ref, route_ref, g_ref, b_ref, o_ref):
    route = route_ref[...]
    moe = route[:, 2:3] * y0_ref[...] + route[:, 3:4] * y1_ref[...]
    o_ref[...] = _layer_norm(ALPHA * x_ref[...] + moe, g_ref[...], b_ref[...])


def _combine(x1, y0, y1, route, ln_g, ln_b):
    t = x1.shape[0]
    ts = SEQ_TILE
    row = lambda c: pl.BlockSpec((ts, c), lambda i: (i, 0))
    return pl.pallas_call(
        _combine_body,
        out_shape=jax.ShapeDtypeStruct((t, D_MODEL), F32),
        grid=(t // ts,),
        in_specs=[row(D_MODEL)] * 3 + [row(128), _const_spec(ln_g.shape), _const_spec(ln_b.shape)],
        out_specs=row(D_MODEL),
        compiler_params=pltpu.CompilerParams(dimension_semantics=("arbitrary",), vmem_limit_bytes=VMEM_LIMIT),
        name="combine",
    )(x1, y0, y1, route, ln_g, ln_b)


def _dispatch(route, counts, t):
    expert = route[:, 0:2].astype(jnp.int32)
    rank = route[:, 4:6].astype(jnp.int32)
    counts = counts[0, :N_EXPERTS].astype(jnp.int32)
    padded = (counts + EXPERT_BLOCK - 1) // EXPERT_BLOCK * EXPERT_BLOCK
    pad_end = jnp.cumsum(padded)
    pad_start = pad_end - padded
    onehot = expert[:, :, None] == jnp.arange(N_EXPERTS, dtype=jnp.int32)[None, None, :]
    dest = jnp.sum(jnp.where(onehot, pad_start[None, None, :], 0), axis=-1) + rank
    n_blocks = (2 * t + EXPERT_BLOCK - 1) // EXPERT_BLOCK + N_EXPERTS
    block_start = jnp.arange(n_blocks, dtype=jnp.int32) * EXPERT_BLOCK
    block_expert = jnp.minimum(jnp.sum((pad_end[None, :] <= block_start[:, None]).astype(jnp.int32), axis=1),
                               N_EXPERTS - 1)
    n_used = (pad_end[-1:] // EXPERT_BLOCK).astype(jnp.int32)
    return dest, block_expert, n_used, n_blocks * EXPERT_BLOCK


def _pad_cols(w, width=LORA_PAD):
    return jnp.pad(w, ((0, 0), (0, width - w.shape[1])))


def _pad_rows(w, height=LORA_PAD):
    return jnp.pad(w, ((0, height - w.shape[0]), (0, 0)))


def _row(v):
    return v.reshape(1, -1).astype(F32)


def _layer_params(l, a):
    w_in = a['w_in_first'] if l == 0 else a['w_in_rest'][l - 1]
    if l == 0:
        w_vres = jnp.zeros((D_MODEL, LORA_PAD), F32)
        mu_vres = jnp.zeros((LORA_PAD,), F32)
    else:
        w_vres = _pad_cols(w_in[:, _O_VRES:])
        mu_vres = jnp.pad(a['rwkv_mu_vres'][l - 1], (0, LORA_PAD - 32))
    mu = a['rwkv_mu'][l]
    p = {
        'w_rw': _bf(jnp.concatenate([w_in[:, :_O_WL], _pad_cols(w_in[:, _O_WL:_O_AL]), _pad_cols(w_in[:, _O_AL:_O_GL]),
                                     _pad_cols(w_in[:, _O_GL:_O_SGU]), w_vres], axis=1)),
        'w_sgu': _bf(w_in[:, _O_SGU:_O_CONV]),
        'w_conv': _bf(w_in[:, _O_CONV:_O_GATE]),
        'w_gate': _bf(w_in[:, _O_GATE:_O_VRES]),
        'mu': _row(jnp.concatenate([mu[:_O_WL], jnp.pad(mu[_O_WL:_O_AL], (0, 96)), jnp.pad(mu[_O_AL:_O_GL], (0, 96)),
                                    jnp.pad(mu[_O_GL:], (0, 32)), mu_vres])),
        'w0': _row(a['rwkv_w0'][l]),
        'w_up': _bf(_pad_rows(a['rwkv_w_up'][l])),
        'a0': _row(a['rwkv_a0'][l]),
        'a_up': _bf(_pad_rows(a['rwkv_a_up'][l])),
        'g_up': _bf(_pad_rows(a['rwkv_g_up'][l])),
        'k_k': _row(a['rwkv_k_k'][l]),
        'k_a': _row(a['rwkv_k_a'][l]),
        'r_k': _row(a['rwkv_r_k'][l]),
        'gn_g': _row(a['rwkv_gn_g'][l]),
        'gn_b': _row(a['rwkv_gn_b'][l]),
        'rwkv_w_o': _bf(a['rwkv_w_o'][l]),
        'sgu_ln_g': _row(a['sgu_ln_g'][l]),
        'sgu_ln_b': _row(a['sgu_ln_b'][l]),
        'sgu_w': jnp.transpose(a['sgu_w'][l], (1, 0, 2)).reshape(SGU_CHUNK, SGU_GROUPS * SGU_CHUNK),
        'sgu_bias': jnp.repeat(jnp.transpose(a['sgu_b'][l]), SGU_W // SGU_GROUPS, axis=1),
        'sgu_w_o': _bf(a['sgu_w_o'][l]),
        'conv_dw': jnp.pad(a['conv_dw'][l], ((0, 1), (0, 0))),
        'conv_db': _row(a['conv_db'][l]),
        'conv_ln_g': _row(a['conv_ln_g'][l]),
        'conv_ln_b': _row(a['conv_ln_b'][l]),
        'conv_w_o': _bf(a['conv_w_o'][l]),
        'w_out': _bf(a['w_out'][l]),
        'ln1_g': _row(a['ln1_g'][l]),
        'ln1_b': _row(a['ln1_b'][l]),
        'ln2_g': _row(a['ln2_g'][l]),
        'ln2_b': _row(a['ln2_b'][l]),
    }
    if l > 0:
        p['v0'] = _row(a['rwkv_v0'][l - 1])
        p['v_up'] = _bf(_pad_rows(a['rwkv_v_up'][l - 1]))
    return p


def _forward(a):
    x = a['x']
    bsz, seq, _ = x.shape
    t = bsz * seq
    x = x.reshape(t, D_MODEL)
    head = jnp.arange(HEAD_SLAB, dtype=jnp.int32) // HEAD_DIM
    head_sum = _bf((head[:, None] == head[None, :]).astype(F32))
    rw = _pad_cols(a['router_w'])
    rw_hi = _bf(rw)
    rw_lo = _bf(rw - rw_hi.astype(F32))
    rw_cat = jnp.concatenate([rw_hi, rw_lo], axis=1)
    rb = _row(jnp.pad(a['router_b'], (0, LORA_PAD - N_EXPERTS)))
    v_first = None
    for l in range(DEPTH):
        p = _layer_params(l, a)
        p.update(head_sum=head_sum, router_hi=rw_hi, router_cat=rw_cat, router_b=rb)
        r, lw, k, v, na, nb, g, ga, part = _mix_in(x, p, v_first, bsz, seq)
        if l == 0:
            v_first = v
        y = _wkv(r, lw, k, v, na, nb, bsz, seq)
        x1, x1b, route, counts = _mix_out(y, r, k, v, g, ga, part, x, p, bsz, seq)
        dest, block_expert, n_used, rows = _dispatch(route, counts, t)
        buf_tok = jnp.zeros((rows,), jnp.int32).at[dest.reshape(2 * t)].set(
            jnp.arange(2 * t, dtype=jnp.int32) // 2, unique_indices=True)
        xg = jnp.take(x1b, buf_tok, axis=0)
        yb = _experts(xg, block_expert, n_used, a['moe_w_gate'], a['moe_w_up'], a['moe_w_down'], l)
        y0 = jnp.take(yb, dest[:, 0], axis=0)
        y1 = jnp.take(yb, dest[:, 1], axis=0)
        x = _combine(x1, y0, y1, route, p['ln2_g'], p['ln2_b'])
    return x.reshape(bsz, seq, D_MODEL)


def kernel(x, w_in_first, w_in_rest, rwkv_mu, rwkv_mu_vres, rwkv_w0, rwkv_w_up, rwkv_a0, rwkv_a_up, rwkv_g_up, rwkv_v0, rwkv_v_up, rwkv_k_k, rwkv_k_a, rwkv_r_k, rwkv_gn_g, rwkv_gn_b, rwkv_w_o, sgu_ln_g, sgu_ln_b, sgu_w, sgu_b, sgu_w_o, conv_dw, conv_db, conv_ln_g, conv_ln_b, conv_w_o, w_out, ln1_g, ln1_b, router_w, router_b, moe_w_gate, moe_w_up, moe_w_down, ln2_g, ln2_b):
    return _forward(dict(
        x=x, w_in_first=w_in_first, w_in_rest=w_in_rest, rwkv_mu=rwkv_mu, rwkv_mu_vres=rwkv_mu_vres,
        rwkv_w0=rwkv_w0, rwkv_w_up=rwkv_w_up, rwkv_a0=rwkv_a0, rwkv_a_up=rwkv_a_up, rwkv_g_up=rwkv_g_up,
        rwkv_v0=rwkv_v0, rwkv_v_up=rwkv_v_up, rwkv_k_k=rwkv_k_k, rwkv_k_a=rwkv_k_a, rwkv_r_k=rwkv_r_k,
        rwkv_gn_g=rwkv_gn_g, rwkv_gn_b=rwkv_gn_b, rwkv_w_o=rwkv_w_o, sgu_ln_g=sgu_ln_g, sgu_ln_b=sgu_ln_b,
        sgu_w=sgu_w, sgu_b=sgu_b, sgu_w_o=sgu_w_o, conv_dw=conv_dw, conv_db=conv_db, conv_ln_g=conv_ln_g,
        conv_ln_b=conv_ln_b, conv_w_o=conv_w_o, w_out=w_out, ln1_g=ln1_g, ln1_b=ln1_b, router_w=router_w,
        router_b=router_b, moe_w_gate=moe_w_gate, moe_w_up=moe_w_up, moe_w_down=moe_w_down, ln2_g=ln2_g,
        ln2_b=ln2_b))
```

```python
import functools

import jax
import jax.numpy as jnp
from jax import lax
from jax.experimental import pallas as pl
from jax.experimental.pallas import tpu as pltpu

F32 = jnp.float32
BF16 = jnp.bfloat16

D_MODEL = 1024
DEPTH = 4
HEADS = 8
HEAD_DIM = 64
RW = HEADS * HEAD_DIM
LORA_PAD = 128
RW_COLS = 3 * RW + 4 * LORA_PAD
SGU_W = 256
SGU_CHUNK = 128
SGU_GROUPS = 4
CONV_W = 256
CONV_K = 31
N_EXPERTS = 32
GROUP_SIZE = 8
N_GROUPS = 4
D_EXPERT = 512
EXPERT_BLOCK = 256
ALPHA = (2 * DEPTH) ** 0.25
LN_EPS = 1e-5
GN_EPS = 64e-5

_O_WL = 3 * RW
_O_AL = _O_WL + 32
_O_GL = _O_AL + 32
_O_SGU = _O_GL + 96
_O_CONV = _O_SGU + 2 * SGU_W
_O_GATE = _O_CONV + 2 * CONV_W
_O_VRES = _O_GATE + 3 * D_MODEL

SEQ_TILE = 256
WKV_CHUNK = 64
WKV_TILE = 256
MIX_OUT_HALVES = 2
PAIR = 2 * HEAD_DIM
HEAD_SLAB = 256
VMEM_LIMIT = 48 * 1024 * 1024


def _dot(a, b):
    return jnp.dot(a, b, preferred_element_type=F32)


def _dot_nt(a, b):
    return lax.dot_general(a, b, (((1,), (1,)), ((), ())), preferred_element_type=F32)


def _bf(x):
    return x.astype(BF16)


def _head_sum(x_bf16, ones_bd):
    w = ones_bd.shape[0]
    return jnp.concatenate([_dot(x_bf16[:, i:i + w], ones_bd) for i in range(0, x_bf16.shape[1], w)], axis=1)


def _head_sum_split(x, ones_bd):
    hi = _bf(x)
    lo = _bf(x - hi.astype(F32))
    return _head_sum(hi, ones_bd) + _head_sum(lo, ones_bd)


def _split_dot_left(w_bf16, x):
    hi = _bf(x)
    lo = _bf(x - hi.astype(F32))
    return _dot(w_bf16, hi) + _dot(w_bf16, lo)


def _layer_norm(x, g, b):
    mu = jnp.mean(x, axis=-1, keepdims=True)
    d = x - mu
    var = jnp.mean(d * d, axis=-1, keepdims=True)
    return d * lax.rsqrt(var + LN_EPS) * g + b


def _sigmoid(x):
    return 1.0 / (1.0 + jnp.exp(-x))


def _const_spec(shape):
    nd = len(shape)
    return pl.BlockSpec(shape, lambda *_: (0,) * nd, pipeline_mode=pl.Buffered(1))


def _mix_in_body(has_vres, ts, *refs):
    it = iter(refs)
    x_ref, wrw_ref, wsgu_ref, wconv_ref, wgate_ref = (next(it) for _ in range(5))
    mu_ref, w0_ref, wup_ref, a0_ref, aup_ref, gup_ref, kk_ref, ka_ref = (next(it) for _ in range(8))
    if has_vres:
        v0_ref, vup_ref, vfirst_ref = (next(it) for _ in range(3))
    hs_ref = next(it)
    slng_ref, slnb_ref, sw_ref, sbias_ref, swo_ref = (next(it) for _ in range(5))
    cdw_ref, cdb_ref, clng_ref, clnb_ref, cwo_ref = (next(it) for _ in range(5))
    r_o, lw_o, k_o, v_o, a_o, b_o, g_o, ga_o, part_o = (next(it) for _ in range(9))
    pbuf, hext, hshift = next(it), next(it), next(it)

    @pl.when(pl.program_id(1) == 0)
    def _():
        pbuf[0:8, :] = jnp.zeros((8, RW_COLS), F32)
        hext[0:32, :] = jnp.zeros((32, CONV_W), F32)

    xb = _bf(x_ref[...])
    pr = _dot(xb, wrw_ref[...])
    pb = _dot(xb, wsgu_ref[...])
    pc = _dot(xb, wconv_ref[...])
    gate_logits = _dot(xb, wgate_ref[...])

    pbuf[8:8 + ts, :] = pr
    prev = pbuf[pl.ds(7, ts), :]
    pa = pr + (prev - pr) * mu_ref[...]
    pbuf[0:8, :] = pbuf[ts:ts + 8, :]
    r = pa[:, 0:RW]
    k = pa[:, RW:2 * RW]
    v = pa[:, 2 * RW:3 * RW]
    wl = pa[:, 3 * RW:3 * RW + LORA_PAD]
    al = pa[:, 3 * RW + LORA_PAD:3 * RW + 2 * LORA_PAD]
    gl = pa[:, 3 * RW + 2 * LORA_PAD:3 * RW + 3 * LORA_PAD]
    z = w0_ref[...] + _dot(_bf(jnp.tanh(wl)), wup_ref[...])
    nz = -z
    softplus = jnp.maximum(nz, 0.0) + jnp.log(1.0 + jnp.exp(-jnp.abs(nz)))
    lw_o[...] = -jnp.exp(-softplus - 0.5)
    iclr = _sigmoid(a0_ref[...] + _dot(_bf(al), aup_ref[...]))
    g_o[...] = _bf(_dot(_bf(_sigmoid(gl)), gup_ref[...]))
    kkv = k * kk_ref[...]
    ss = _head_sum(_bf(kkv * kkv), hs_ref[...])
    kkn = kkv / jnp.maximum(jnp.sqrt(ss), 1e-12)
    k_o[...] = _bf(k * (1.0 + (iclr - 1.0) * ka_ref[...]))
    if has_vres:
        vl = pa[:, 3 * RW + 3 * LORA_PAD:RW_COLS]
        v = v + (vfirst_ref[...] - v) * _sigmoid(v0_ref[...] + _dot(_bf(vl), vup_ref[...]))
    r_o[...] = _bf(r)
    v_o[...] = _bf(v)
    a_o[...] = _bf(-kkn)
    b_o[...] = _bf(kkn * iclr)

    zz = 0.5 * pb * (1.0 + lax.erf(pb * (2.0 ** -0.5)))
    u = zz[:, :SGU_W]
    vv = _layer_norm(zz[:, SGU_W:], slng_ref[...], slnb_ref[...])
    prow = lax.broadcasted_iota(jnp.int32, (SGU_CHUNK, SGU_GROUPS * SGU_CHUNK), 0)
    qcol = lax.broadcasted_iota(jnp.int32, (SGU_CHUNK, SGU_GROUPS * SGU_CHUNK), 1) % SGU_CHUNK
    wc = _bf(jnp.where(qcol <= prow, sw_ref[...], 0.0))
    lane_grp = lax.broadcasted_iota(jnp.int32, (SGU_CHUNK, SGU_W), 1) // (SGU_W // SGU_GROUPS)
    mixed = []
    for c in range(ts // SGU_CHUNK):
        vc = vv[c * SGU_CHUNK:(c + 1) * SGU_CHUNK]
        stack = jnp.concatenate([jnp.where(lane_grp == g, vc, 0.0) for g in range(SGU_GROUPS)], axis=0)
        mixed.append(_dot(wc, _bf(stack)) + sbias_ref[...])
    mixed = jnp.concatenate(mixed, axis=0)
    o_b = _dot(_bf(u * mixed), swo_ref[...])

    hext[32:32 + ts, :] = pc[:, :CONV_W] * _sigmoid(pc[:, CONV_W:])
    acc = jnp.zeros((ts, CONV_W), F32) + cdb_ref[...]
    first = 32 - (CONV_K - 1)
    for rem in range(8):
        taps = [j for j in range(CONV_K) if (first + j) % 8 == rem]
        span = max(first + j for j in taps) - rem
        hshift[0:span + ts, :] = hext[pl.ds(rem, span + ts), :]
        for j in taps:
            off = first + j - rem
            acc = acc + cdw_ref[j:j + 1, :] * hshift[off:off + ts, :]
    hext[0:32, :] = hext[ts:ts + 32, :]
    hc = _layer_norm(acc, clng_ref[...], clnb_ref[...])
    hc = hc * _sigmoid(hc)
    o_c = _dot(_bf(hc), cwo_ref[...])

    gates = _sigmoid(_bf(gate_logits))
    ga_o[...] = gates[:, :D_MODEL]
    part_o[...] = _bf(gates[:, D_MODEL:2 * D_MODEL] * o_b + gates[:, 2 * D_MODEL:] * o_c)


def _mix_in(x, p, v_first, bsz, seq):
    t = bsz * seq
    ts = SEQ_TILE
    nst = seq // ts
    has_vres = v_first is not None
    row = lambda c: pl.BlockSpec((ts, c), lambda b, s: (b * nst + s, 0))
    args = [x, p['w_rw'], p['w_sgu'], p['w_conv'], p['w_gate'], p['mu'], p['w0'], p['w_up'], p['a0'], p['a_up'],
            p['g_up'], p['k_k'], p['k_a']]
    specs = [row(D_MODEL)] + [_const_spec(a.shape) for a in args[1:]]
    if has_vres:
        args += [p['v0'], p['v_up'], v_first]
        specs += [_const_spec(p['v0'].shape), _const_spec(p['v_up'].shape), row(RW)]
    tail = [p['head_sum'], p['sgu_ln_g'], p['sgu_ln_b'], p['sgu_w'], p['sgu_bias'], p['sgu_w_o'],
            p['conv_dw'], p['conv_db'], p['conv_ln_g'], p['conv_ln_b'], p['conv_w_o']]
    args += tail
    specs += [_const_spec(a.shape) for a in tail]
    out_shape = ([jax.ShapeDtypeStruct((t, RW), F32 if i == 1 else BF16) for i in range(7)]
                 + [jax.ShapeDtypeStruct((t, D_MODEL), BF16)] * 2)
    out_specs = [row(RW)] * 7 + [row(D_MODEL)] * 2
    return pl.pallas_call(
        functools.partial(_mix_in_body, has_vres, ts),
        out_shape=out_shape,
        grid=(bsz, nst),
        in_specs=specs,
        out_specs=out_specs,
        scratch_shapes=[pltpu.VMEM((ts + 8, RW_COLS), F32), pltpu.VMEM((ts + 32, CONV_W), F32),
                        pltpu.VMEM((ts + 32, CONV_W), F32)],
        compiler_params=pltpu.CompilerParams(dimension_semantics=("arbitrary", "arbitrary"),
                                             vmem_limit_bytes=VMEM_LIMIT),
        name="mix_in",
    )(*args)


def _wkv_body(tb, r_ref, lw_ref, k_ref, v_ref, a_ref, b_ref, y_ref, st_ref, wr_s, rs_s, ar_s, vs_s, kb_s, dc_s):
    c = WKV_CHUNK
    nc = tb // c
    npair = HEADS // 2
    units = [(ci, p) for ci in range(nc) for p in range(npair)]

    @pl.when(pl.program_id(1) == 0)
    def _():
        st_ref[...] = jnp.zeros(st_ref.shape, F32)

    lane = lax.broadcasted_iota(jnp.int32, (c, PAIR), 1)
    head0 = lane < HEAD_DIM
    ti = lax.broadcasted_iota(jnp.int32, (PAIR, PAIR), 0)
    si = lax.broadcasted_iota(jnp.int32, (PAIR, PAIR), 1)
    same = (ti // c) == (si // c)
    lower = same & ((si % c) <= (ti % c))
    strict = same & ((si % c) < (ti % c))
    eye = (ti == si).astype(F32)
    tr = lax.broadcasted_iota(jnp.int32, (tb, tb), 0)
    tc = lax.broadcasted_iota(jnp.int32, (tb, tb), 1)
    tri = _bf(((tr // c == tc // c) & (tc <= tr)).astype(F32))

    def stack(x):
        return jnp.concatenate([jnp.where(head0, x, 0.0), jnp.where(head0, 0.0, x)], axis=0)

    lw = lw_ref[...]
    cum = _split_dot_left(tri, lw)
    cum_end = jnp.concatenate(
        [jnp.broadcast_to(cum[(ci + 1) * c - 1:(ci + 1) * c, :], (c, RW)) for ci in range(nc)], axis=0)
    p_inv = jnp.exp(-cum)
    p_end = jnp.exp(cum_end - cum)
    rt = r_ref[...] * jnp.exp(cum)
    at = a_ref[...] * jnp.exp(cum - lw)
    kt = k_ref[...] * p_inv
    bt = b_ref[...] * p_inv
    ke = k_ref[...] * p_end
    be = b_ref[...] * p_end
    vt = v_ref[...]

    def sub(x, ci, p):
        return x[ci * c:(ci + 1) * c, p * PAIR:(p + 1) * PAIR]

    a_s = [stack(sub(at, ci, p)) for ci, p in units]
    r_s = [stack(sub(rt, ci, p)) for ci, p in units]
    g = [_dot_nt(_bf(jnp.concatenate([r_s[i], a_s[i]], axis=0)),
                 _bf(jnp.concatenate([stack(sub(kt, ci, p)), stack(sub(bt, ci, p))], axis=0)))
         for i, (ci, p) in enumerate(units)]
    for i, (ci, p) in enumerate(units):
        vs_s[ci, p] = _bf(stack(sub(vt, ci, p)))
        ar_s[ci, p] = _bf(jnp.concatenate([jnp.where(lower, g[i][0:PAIR, 0:PAIR], 0.0),
                                           jnp.where(lower, g[i][0:PAIR, PAIR:], 0.0)], axis=1))
        kb_s[ci, p] = _bf(jnp.concatenate([stack(sub(ke, ci, p)).T, stack(sub(be, ci, p)).T], axis=1))
        dc_s[ci, p] = jnp.exp(jnp.broadcast_to(sub(cum_end, ci, p)[0:1, :], (PAIR, PAIR))).T
    apow = [jnp.where(strict, g[i][PAIR:, PAIR:], 0.0) for i in range(len(units))]
    inv = [eye + x for x in apow]
    apb = [_bf(x) for x in apow]
    apow = [_dot(x, x) for x in apb]
    for _ in range(4):
        apb = [_bf(x) for x in apow]
        both = [_dot(x, jnp.concatenate([x, _bf(y)], axis=1)) for x, y in zip(apb, inv)]
        apow = [x[:, :PAIR] for x in both]
        inv = [y + x[:, PAIR:] for x, y in zip(both, inv)]
    inv = [y + _dot(_bf(x), _bf(y)) for x, y in zip(apow, inv)]
    for i, (ci, p) in enumerate(units):
        a_ak = jnp.where(strict, g[i][PAIR:, 0:PAIR], 0.0)
        wr_s[ci, p] = _bf(_dot(_bf(inv[i]), _bf(jnp.concatenate([a_s[i], a_ak], axis=1))))
        rs_s[ci, p] = _bf(r_s[i])

    s = [st_ref[p] for p in range(npair)]
    for ci in range(nc):
        sb = [_bf(x) for x in s]
        u = [_dot(wr_s[ci, p], jnp.concatenate([sb[p], vs_s[ci, p]], axis=0)) for p in range(npair)]
        vu = [jnp.concatenate([vs_s[ci, p], _bf(u[p])], axis=0) for p in range(npair)]
        s = [dc_s[ci, p] * s[p] + _dot(kb_s[ci, p], vu[p]) for p in range(npair)]
        for p in range(npair):
            y = _dot(rs_s[ci, p], sb[p]) + _dot(ar_s[ci, p], vu[p])
            y_ref[ci * c:(ci + 1) * c, p * PAIR:(p + 1) * PAIR] = y[0:c] + y[c:]
    for p in range(npair):
        st_ref[p] = s[p]


def _wkv(r, lw, k, v, a, b, bsz, seq):
    t = bsz * seq
    tb = WKV_TILE
    nst = seq // tb
    nc, npair = tb // WKV_CHUNK, HEADS // 2
    row = pl.BlockSpec((tb, RW), lambda bi, s: (bi * nst + s, 0))
    return pl.pallas_call(
        functools.partial(_wkv_body, tb),
        out_shape=jax.ShapeDtypeStruct((t, RW), F32),
        grid=(bsz, nst),
        in_specs=[row] * 6,
        out_specs=row,
        scratch_shapes=[pltpu.VMEM((npair, PAIR, PAIR), F32),
                        pltpu.VMEM((nc, npair, PAIR, 2 * PAIR), BF16),
                        pltpu.VMEM((nc, npair, PAIR, PAIR), BF16),
                        pltpu.VMEM((nc, npair, PAIR, 2 * PAIR), BF16),
                        pltpu.VMEM((nc, npair, PAIR, PAIR), BF16),
                        pltpu.VMEM((nc, npair, PAIR, 2 * PAIR), BF16),
                        pltpu.VMEM((nc, npair, PAIR, PAIR), F32)],
        compiler_params=pltpu.CompilerParams(dimension_semantics=("arbitrary", "arbitrary"),
                                             vmem_limit_bytes=VMEM_LIMIT),
        name="wkv",
    )(r, lw, k, v, a, b)


def _mix_out_body(halves, y_ref, r_ref, k_ref, v_ref, g_ref, ga_ref, part_ref, x_ref, hs_ref, rk_ref, gng_ref,
                  gnb_ref, wo_ref, wout_ref, ln_g_ref, ln_b_ref, rw_hi_ref, rw_cat_ref, rb_ref, x1_o, x1b_o, route_o,
                  route_t_o, counts_o, cnt_ref):
    th = y_ref.shape[0] // halves
    hrows = [pl.ds(h * th, th) for h in range(halves)]

    def each(fn, *cols):
        return [fn(*a) for a in zip(*cols)]

    def load(ref):
        return [ref[rs, :] for rs in hrows]

    hs = hs_ref[...]
    inv_n = 1.0 / HEAD_DIM
    y = load(y_ref)
    mu = each(lambda a: _head_sum_split(a, hs) * inv_n, y)
    d = each(lambda a, m: a - m, y, mu)
    var = each(lambda a: _head_sum(_bf(a * a), hs) * inv_n, d)
    rkk = each(lambda a, b: _head_sum(_bf(a * b * rk_ref[...]), hs), load(r_ref), load(k_ref))
    yn = each(lambda a, s2: a * lax.rsqrt(s2 + GN_EPS) * gng_ref[...] + gnb_ref[...], d, var)
    out = each(lambda a, bo, vv, gg: _bf((a + bo * vv) * gg), yn, rkk, load(v_ref), load(g_ref))
    o_a = each(lambda a: _dot(a, wo_ref[...]), out)
    merged = each(lambda a, b, c2: _bf(a * b + c2), load(ga_ref), o_a, load(part_ref))
    xm = each(lambda a: _dot(a, wout_ref[...]), merged)
    x1 = each(lambda a, b: _layer_norm(ALPHA * a + b, ln_g_ref[...], ln_b_ref[...]), load(x_ref), xm)
    xh = each(_bf, x1)
    for rs, a, b in zip(hrows, x1, xh):
        x1_o[rs, :] = a
        x1b_o[rs, :] = b

    xl = each(lambda a, b: _bf(a - b.astype(F32)), x1, xh)
    hcat = each(lambda a: _dot(a, rw_cat_ref[...]), xh)
    logits = each(lambda c2, b: c2[:, :128] + c2[:, 128:] + _dot(b, rw_hi_ref[...]), hcat, xl)
    scores = each(_sigmoid, logits)
    sel = each(lambda a: a + rb_ref[...], scores)
    lane = lax.broadcasted_iota(jnp.int32, (th, 128), 1).astype(F32)
    neg = jnp.float32(-jnp.inf)

    def top2(sg):
        m1 = jnp.max(sg, axis=-1, keepdims=True)
        i1 = jnp.min(jnp.where(sg == m1, lane, 1e9), axis=-1, keepdims=True)
        sg2 = jnp.where(lane == i1, neg, sg)
        m2 = jnp.max(sg2, axis=-1, keepdims=True)
        i2 = jnp.min(jnp.where(sg2 == m2, lane, 1e9), axis=-1, keepdims=True)
        return m1 + m2, i1, i2

    best = e0 = e1 = None
    for g in range(N_GROUPS):
        in_g = (lane >= g * GROUP_SIZE) & (lane < (g + 1) * GROUP_SIZE)
        cand = each(lambda a: top2(jnp.where(in_g, a, neg)), sel)
        if g == 0:
            best, e0, e1 = ([c[i] for c in cand] for i in range(3))
        else:
            take = each(lambda c, b: c[0] > b, cand, best)
            best = each(lambda tk, c, b: jnp.where(tk, c[0], b), take, cand, best)
            e0 = each(lambda tk, c, b: jnp.where(tk, c[1], b), take, cand, e0)
            e1 = each(lambda tk, c, b: jnp.where(tk, c[2], b), take, cand, e1)
    g0 = each(lambda e, sc: jnp.sum(jnp.where(lane == e, sc, 0.0), axis=-1, keepdims=True), e0, scores)
    g1 = each(lambda e, sc: jnp.sum(jnp.where(lane == e, sc, 0.0), axis=-1, keepdims=True), e1, scores)

    @pl.when(pl.program_id(0) == 0)
    def _():
        cnt_ref[...] = jnp.zeros(cnt_ref.shape, F32)

    chosen = each(lambda a, b: jnp.where((lane == a) | (lane == b), 1.0, 0.0), e0, e1)
    before = _bf((lax.broadcasted_iota(jnp.int32, (th, th), 1)
                  < lax.broadcasted_iota(jnp.int32, (th, th), 0)).astype(F32))
    inside = each(lambda a: _dot(before, _bf(a)), chosen)
    total = each(lambda a: jnp.sum(a, axis=0, keepdims=True), chosen)
    base = cnt_ref[...]
    for h in range(halves):
        seen = inside[h] + base
        rank0 = jnp.sum(jnp.where(lane == e0[h], seen, 0.0), axis=-1, keepdims=True)
        rank1 = jnp.sum(jnp.where(lane == e1[h], seen, 0.0), axis=-1, keepdims=True)
        tot = g0[h] + g1[h]
        route = jnp.where(
            lane == 0, e0[h], jnp.where(lane == 1, e1[h], jnp.where(lane == 2, g0[h] / tot, jnp.where(
                lane == 3, g1[h] / tot, jnp.where(lane == 4, rank0, jnp.where(lane == 5, rank1, 0.0))))))
        route_o[hrows[h], :] = route
        route_t_o[:, hrows[h]] = route.T[0:8, :]
        base = base + total[h]
    cnt_ref[...] = base
    counts_o[...] = base


def _mix_out(y, r, k, v, g, ga, part, x, p, bsz, seq):
    t = bsz * seq
    ts = MIX_OUT_HALVES * SEQ_TILE
    row = lambda c: pl.BlockSpec((ts, c), lambda i: (i, 0))
    consts = [p['head_sum'], p['r_k'], p['gn_g'], p['gn_b'], p['rwkv_w_o'], p['w_out'], p['ln1_g'], p['ln1_b'],
              p['router_hi'], p['router_cat'], p['router_b']]
    return pl.pallas_call(
        functools.partial(_mix_out_body, MIX_OUT_HALVES),
        out_shape=[jax.ShapeDtypeStruct((t, D_MODEL), F32), jax.ShapeDtypeStruct((t, D_MODEL), BF16),
                   jax.ShapeDtypeStruct((t, 128), F32), jax.ShapeDtypeStruct((8, t), F32),
                   jax.ShapeDtypeStruct((1, 128), F32)],
        grid=(t // ts,),
        in_specs=[row(RW)] * 5 + [row(D_MODEL)] * 3 + [_const_spec(a.shape) for a in consts],
        out_specs=[row(D_MODEL), row(D_MODEL), row(128), pl.BlockSpec((8, ts), lambda i: (0, i)),
                   pl.BlockSpec((1, 128), lambda i: (0, 0))],
        scratch_shapes=[pltpu.VMEM((1, 128), F32)],
        compiler_params=pltpu.CompilerParams(dimension_semantics=("arbitrary",), vmem_limit_bytes=VMEM_LIMIT),
        name="mix_out",
    )(y, r, k, v, g, ga, part, x, *consts)


def _experts_body(be_ref, nb_ref, x_ref, wg_ref, wu_ref, wd_ref, o_ref, wg_s, wu_s, wd_s):
    i = pl.program_id(0)
    used = i < nb_ref[0]

    @pl.when(used & ((i == 0) | (be_ref[i] != be_ref[jnp.maximum(i - 1, 0)])))
    def _():
        wg_s[...] = _bf(wg_ref[...])
        wu_s[...] = _bf(wu_ref[...])
        wd_s[...] = _bf(wd_ref[...])

    @pl.when(used)
    def _():
        hb = EXPERT_BLOCK // 2
        rows = [pl.ds(h * hb, hb) for h in range(2)]
        xs = [x_ref[rs, :] for rs in rows]
        hg = [_dot(a, wg_s[...]) for a in xs]
        hu = [_dot(a, wu_s[...]) for a in xs]
        hh = [_bf(a * _sigmoid(a) * b) for a, b in zip(hg, hu)]
        out = [_dot(a, wd_s[...]) for a in hh]
        for rs, a in zip(rows, out):
            o_ref[rs, :] = _bf(a)

    @pl.when(jnp.logical_not(used))
    def _():
        o_ref[...] = jnp.zeros(o_ref.shape, BF16)


def _experts(xg, block_expert, n_used, wg, wu, wd, layer):
    rows = xg.shape[0]
    nb = rows // EXPERT_BLOCK
    grid_spec = pltpu.PrefetchScalarGridSpec(
        num_scalar_prefetch=2,
        grid=(nb,),
        in_specs=[
            pl.BlockSpec((EXPERT_BLOCK, D_MODEL), lambda i, be, nu: (i, 0)),
            pl.BlockSpec((None, None, D_MODEL, D_EXPERT), lambda i, be, nu: (layer, be[i], 0, 0)),
            pl.BlockSpec((None, None, D_MODEL, D_EXPERT), lambda i, be, nu: (layer, be[i], 0, 0)),
            pl.BlockSpec((None, None, D_EXPERT, D_MODEL), lambda i, be, nu: (layer, be[i], 0, 0)),
        ],
        out_specs=pl.BlockSpec((EXPERT_BLOCK, D_MODEL), lambda i, be, nu: (i, 0)),
        scratch_shapes=[pltpu.VMEM((D_MODEL, D_EXPERT), BF16), pltpu.VMEM((D_MODEL, D_EXPERT), BF16),
                        pltpu.VMEM((D_EXPERT, D_MODEL), BF16)],
    )
    return pl.pallas_call(
        _experts_body,
        out_shape=jax.ShapeDtypeStruct((rows, D_MODEL), BF16),
        grid_spec=grid_spec,
        compiler_params=pltpu.CompilerParams(dimension_semantics=("arbitrary",), vmem_limit_bytes=VMEM_LIMIT),
        name="experts",
    )(block_expert, n_used, xg, wg, wu, wd)


def _combine_body(x_ref, y0_ref, y1_ref, route_ref, g_ref, b_ref, o_ref):
    route = route_ref[...]
    moe = route[:, 2:3] * y0_ref[...] + route[:, 3:4] * y1_ref[...]
    o_ref[...] = _layer_norm(ALPHA * x_ref[...] + moe, g_ref[...], b_ref[...])


def _combine(x1, y0, y1, route, ln_g, ln_b):
    t = x1.shape[0]
    ts = SEQ_TILE
    row = lambda c: pl.BlockSpec((ts, c), lambda i: (i, 0))
    return pl.pallas_call(
        _combine_body,
        out_shape=jax.ShapeDtypeStruct((t, D_MODEL), F32),
        grid=(t // ts,),
        in_specs=[row(D_MODEL)] * 3 + [row(128), _const_spec(ln_g.shape), _const_spec(ln_b.shape)],
        out_specs=row(D_MODEL),
        compiler_params=pltpu.CompilerParams(dimension_semantics=("arbitrary",), vmem_limit_bytes=VMEM_LIMIT),
        name="combine",
    )(x1, y0, y1, route, ln_g, ln_b)


def _dispatch(route_t, counts, t):
    expert = route_t[0:2].astype(jnp.int32)
    rank = route_t[4:6].astype(jnp.int32)
    counts = counts[0, :N_EXPERTS].astype(jnp.int32)
    padded = (counts + EXPERT_BLOCK - 1) // EXPERT_BLOCK * EXPERT_BLOCK
    pad_end = jnp.cumsum(padded)
    pad_start = pad_end - padded
    dest = rank
    for e in range(N_EXPERTS):
        dest = dest + jnp.where(expert == e, pad_start[e], 0)
    n_blocks = (2 * t + EXPERT_BLOCK - 1) // EXPERT_BLOCK + N_EXPERTS
    rows = n_blocks * EXPERT_BLOCK
    tok = jnp.arange(t, dtype=jnp.int32)
    buf_tok = (jnp.arange(rows, dtype=jnp.int32) % t).at[dest.reshape(2 * t)].set(
        jnp.concatenate([tok, tok]), unique_indices=True)
    block_start = jnp.arange(n_blocks, dtype=jnp.int32) * EXPERT_BLOCK
    block_expert = jnp.minimum(jnp.sum((pad_end[None, :] <= block_start[:, None]).astype(jnp.int32), axis=1),
                               N_EXPERTS - 1)
    n_used = (pad_end[-1:] // EXPERT_BLOCK).astype(jnp.int32)
    return dest, buf_tok, block_expert, n_used


def _pad_cols(w, width=LORA_PAD):
    return jnp.pad(w, ((0, 0), (0, width - w.shape[1])))


def _pad_rows(w, height=LORA_PAD):
    return jnp.pad(w, ((0, height - w.shape[0]), (0, 0)))


def _row(v):
    return v.reshape(1, -1).astype(F32)


def _layer_params(l, a):
    w_in = a['w_in_first'] if l == 0 else a['w_in_rest'][l - 1]
    if l == 0:
        w_vres = jnp.zeros((D_MODEL, LORA_PAD), F32)
        mu_vres = jnp.zeros((LORA_PAD,), F32)
    else:
        w_vres = _pad_cols(w_in[:, _O_VRES:])
        mu_vres = jnp.pad(a['rwkv_mu_vres'][l - 1], (0, LORA_PAD - 32))
    mu = a['rwkv_mu'][l]
    p = {
        'w_rw': _bf(jnp.concatenate([w_in[:, :_O_WL], _pad_cols(w_in[:, _O_WL:_O_AL]), _pad_cols(w_in[:, _O_AL:_O_GL]),
                                     _pad_cols(w_in[:, _O_GL:_O_SGU]), w_vres], axis=1)),
        'w_sgu': _bf(w_in[:, _O_SGU:_O_CONV]),
        'w_conv': _bf(w_in[:, _O_CONV:_O_GATE]),
        'w_gate': _bf(w_in[:, _O_GATE:_O_VRES]),
        'mu': _row(jnp.concatenate([mu[:_O_WL], jnp.pad(mu[_O_WL:_O_AL], (0, 96)), jnp.pad(mu[_O_AL:_O_GL], (0, 96)),
                                    jnp.pad(mu[_O_GL:], (0, 32)), mu_vres])),
        'w0': _row(a['rwkv_w0'][l]),
        'w_up': _bf(_pad_rows(a['rwkv_w_up'][l])),
        'a0': _row(a['rwkv_a0'][l]),
        'a_up': _bf(_pad_rows(a['rwkv_a_up'][l])),
        'g_up': _bf(_pad_rows(a['rwkv_g_up'][l])),
        'k_k': _row(a['rwkv_k_k'][l]),
        'k_a': _row(a['rwkv_k_a'][l]),
        'r_k': _row(a['rwkv_r_k'][l]),
        'gn_g': _row(a['rwkv_gn_g'][l]),
        'gn_b': _row(a['rwkv_gn_b'][l]),
        'rwkv_w_o': _bf(a['rwkv_w_o'][l]),
        'sgu_ln_g': _row(a['sgu_ln_g'][l]),
        'sgu_ln_b': _row(a['sgu_ln_b'][l]),
        'sgu_w': jnp.transpose(a['sgu_w'][l], (1, 0, 2)).reshape(SGU_CHUNK, SGU_GROUPS * SGU_CHUNK),
        'sgu_bias': jnp.repeat(jnp.transpose(a['sgu_b'][l]), SGU_W // SGU_GROUPS, axis=1),
        'sgu_w_o': _bf(a['sgu_w_o'][l]),
        'conv_dw': jnp.pad(a['conv_dw'][l], ((0, 1), (0, 0))),
        'conv_db': _row(a['conv_db'][l]),
        'conv_ln_g': _row(a['conv_ln_g'][l]),
        'conv_ln_b': _row(a['conv_ln_b'][l]),
        'conv_w_o': _bf(a['conv_w_o'][l]),
        'w_out': _bf(a['w_out'][l]),
        'ln1_g': _row(a['ln1_g'][l]),
        'ln1_b': _row(a['ln1_b'][l]),
        'ln2_g': _row(a['ln2_g'][l]),
        'ln2_b': _row(a['ln2_b'][l]),
    }
    if l > 0:
        p['v0'] = _row(a['rwkv_v0'][l - 1])
        p['v_up'] = _bf(_pad_rows(a['rwkv_v_up'][l - 1]))
    return p


def _forward(a):
    x = a['x']
    bsz, seq, _ = x.shape
    t = bsz * seq
    x = x.reshape(t, D_MODEL)
    head = jnp.arange(HEAD_SLAB, dtype=jnp.int32) // HEAD_DIM
    head_sum = _bf((head[:, None] == head[None, :]).astype(F32))
    rw = _pad_cols(a['router_w'])
    rw_hi = _bf(rw)
    rw_lo = _bf(rw - rw_hi.astype(F32))
    rw_cat = jnp.concatenate([rw_hi, rw_lo], axis=1)
    rb = _row(jnp.pad(a['router_b'], (0, LORA_PAD - N_EXPERTS)))
    v_first = None
    for l in range(DEPTH):
        p = _layer_params(l, a)
        p.update(head_sum=head_sum, router_hi=rw_hi, router_cat=rw_cat, router_b=rb)
        r, lw, k, v, na, nb, g, ga, part = _mix_in(x, p, v_first, bsz, seq)
        if l == 0:
            v_first = v
        y = _wkv(r, lw, k, v, na, nb, bsz, seq)
        x1, x1b, route, route_t, counts = _mix_out(y, r, k, v, g, ga, part, x, p, bsz, seq)
        dest, buf_tok, block_expert, n_used = _dispatch(route_t, counts, t)
        xg = jnp.take(x1b, buf_tok, axis=0)
        yb = _experts(xg, block_expert, n_used, a['moe_w_gate'], a['moe_w_up'], a['moe_w_down'], l)
        y0 = jnp.take(yb, dest[0], axis=0)
        y1 = jnp.take(yb, dest[1], axis=0)
        x = _combine(x1, y0, y1, route, p['ln2_g'], p['ln2_b'])
    return x.reshape(bsz, seq, D_MODEL)


def kernel(x, w_in_first, w_in_rest, rwkv_mu, rwkv_mu_vres, rwkv_w0, rwkv_w_up, rwkv_a0, rwkv_a_up, rwkv_g_up, rwkv_v0, rwkv_v_up, rwkv_k_k, rwkv_k_a, rwkv_r_k, rwkv_gn_g, rwkv_gn_b, rwkv_w_o, sgu_ln_g, sgu_ln_b, sgu_w, sgu_b, sgu_w_o, conv_dw, conv_db, conv_ln_g, conv_ln_b, conv_w_o, w_out, ln1_g, ln1_b, router_w, router_b, moe_w_gate, moe_w_up, moe_w_down, ln2_g, ln2_b):
    return _forward(dict(
        x=x, w_in_first=w_in_first, w_in_rest=w_in_rest, rwkv_mu=rwkv_mu, rwkv_mu_vres=rwkv_mu_vres,
        rwkv_w0=rwkv_w0, rwkv_w_up=rwkv_w_up, rwkv_a0=rwkv_a0, rwkv_a_up=rwkv_a_up, rwkv_g_up=rwkv_g_up,
        rwkv_v0=rwkv_v0, rwkv_v_up=rwkv_v_up, rwkv_k_k=rwkv_k_k, rwkv_k_a=rwkv_k_a, rwkv_r_k=rwkv_r_k,
        rwkv_gn_g=rwkv_gn_g, rwkv_gn_b=rwkv_gn_b, rwkv_w_o=rwkv_w_o, sgu_ln_g=sgu_ln_g, sgu_ln_b=sgu_ln_b,
        sgu_w=sgu_w, sgu_b=sgu_b, sgu_w_o=sgu_w_o, conv_dw=conv_dw, conv_db=conv_db, conv_ln_g=conv_ln_g,
        conv_ln_b=conv_ln_b, conv_w_o=conv_w_o, w_out=w_out, ln1_g=ln1_g, ln1_b=ln1_b, router_w=router_w,
        router_b=router_b, moe_w_gate=moe_w_gate, moe_w_up=moe_w_up, moe_w_down=moe_w_down, ln2_g=ln2_g,
        ln2_b=ln2_b))
```

```python
import functools

import jax
import jax.numpy as jnp
from jax import lax
from jax.experimental import pallas as pl
from jax.experimental.pallas import tpu as pltpu

F32 = jnp.float32
BF16 = jnp.bfloat16

D_MODEL = 1024
DEPTH = 4
HEADS = 8
HEAD_DIM = 64
RW = HEADS * HEAD_DIM
LORA_PAD = 128
RW_COLS = 3 * RW + 4 * LORA_PAD
SGU_W = 256
SGU_CHUNK = 128
SGU_GROUPS = 4
CONV_W = 256
CONV_K = 31
N_EXPERTS = 32
GROUP_SIZE = 8
N_GROUPS = 4
D_EXPERT = 512
EXPERT_BLOCK = 512
ALPHA = (2 * DEPTH) ** 0.25
LN_EPS = 1e-5
GN_EPS = 64e-5

_O_WL = 3 * RW
_O_AL = _O_WL + 32
_O_GL = _O_AL + 32
_O_SGU = _O_GL + 96
_O_CONV = _O_SGU + 2 * SGU_W
_O_GATE = _O_CONV + 2 * CONV_W
_O_VRES = _O_GATE + 3 * D_MODEL

SEQ_TILE = 256
WKV_CHUNK = 64
WKV_TILE = 256
MIX_OUT_HALVES = 2
COMBINE_TILE = 512
PAIR = 2 * HEAD_DIM
HEAD_SLAB = 256
VMEM_LIMIT = 48 * 1024 * 1024


def _dot(a, b):
    return jnp.dot(a, b, preferred_element_type=F32)


def _dot_nt(a, b):
    return lax.dot_general(a, b, (((1,), (1,)), ((), ())), preferred_element_type=F32)


def _bf(x):
    return x.astype(BF16)


def _head_sum(x_bf16, ones_bd):
    w = ones_bd.shape[0]
    return jnp.concatenate([_dot(x_bf16[:, i:i + w], ones_bd) for i in range(0, x_bf16.shape[1], w)], axis=1)


def _head_sum_split(x, ones_bd):
    hi = _bf(x)
    lo = _bf(x - hi.astype(F32))
    return _head_sum(hi, ones_bd) + _head_sum(lo, ones_bd)


def _split_dot_left(w_bf16, x):
    hi = _bf(x)
    lo = _bf(x - hi.astype(F32))
    return _dot(w_bf16, hi) + _dot(w_bf16, lo)


def _layer_norm(x, g, b):
    mu = jnp.mean(x, axis=-1, keepdims=True)
    d = x - mu
    var = jnp.mean(d * d, axis=-1, keepdims=True)
    return d * lax.rsqrt(var + LN_EPS) * g + b


def _sigmoid(x):
    return 1.0 / (1.0 + jnp.exp(-x))


def _const_spec(shape):
    nd = len(shape)
    return pl.BlockSpec(shape, lambda *_: (0,) * nd, pipeline_mode=pl.Buffered(1))


def _mix_in_body(has_vres, ts, *refs):
    it = iter(refs)
    x_ref, wrw_ref, wsgu_ref, wconv_ref, wgate_ref = (next(it) for _ in range(5))
    mu_ref, w0_ref, wup_ref, a0_ref, aup_ref, gup_ref, kk_ref, ka_ref = (next(it) for _ in range(8))
    if has_vres:
        v0_ref, vup_ref, vfirst_ref = (next(it) for _ in range(3))
    hs_ref = next(it)
    slng_ref, slnb_ref, sw_ref, sbias_ref, swo_ref = (next(it) for _ in range(5))
    cdw_ref, cdb_ref, clng_ref, clnb_ref, cwo_ref = (next(it) for _ in range(5))
    r_o, lw_o, k_o, v_o, a_o, b_o, g_o, ga_o, part_o = (next(it) for _ in range(9))
    pbuf, hext, hshift = next(it), next(it), next(it)

    @pl.when(pl.program_id(1) == 0)
    def _():
        pbuf[0:8, :] = jnp.zeros((8, RW_COLS), F32)
        hext[0:32, :] = jnp.zeros((32, CONV_W), F32)

    xb = _bf(x_ref[...])
    pr = _dot(xb, wrw_ref[...])
    pb = _dot(xb, wsgu_ref[...])
    pc = _dot(xb, wconv_ref[...])
    gate_logits = _dot(xb, wgate_ref[...])

    pbuf[8:8 + ts, :] = pr
    prev = pbuf[pl.ds(7, ts), :]
    pa = pr + (prev - pr) * mu_ref[...]
    pbuf[0:8, :] = pbuf[ts:ts + 8, :]
    r = pa[:, 0:RW]
    k = pa[:, RW:2 * RW]
    v = pa[:, 2 * RW:3 * RW]
    wl = pa[:, 3 * RW:3 * RW + LORA_PAD]
    al = pa[:, 3 * RW + LORA_PAD:3 * RW + 2 * LORA_PAD]
    gl = pa[:, 3 * RW + 2 * LORA_PAD:3 * RW + 3 * LORA_PAD]
    z = w0_ref[...] + _dot(_bf(jnp.tanh(wl)), wup_ref[...])
    nz = -z
    softplus = jnp.maximum(nz, 0.0) + jnp.log(1.0 + jnp.exp(-jnp.abs(nz)))
    lw_o[...] = -jnp.exp(-softplus - 0.5)
    iclr = _sigmoid(a0_ref[...] + _dot(_bf(al), aup_ref[...]))
    g_o[...] = _bf(_dot(_bf(_sigmoid(gl)), gup_ref[...]))
    kkv = k * kk_ref[...]
    ss = _head_sum(_bf(kkv * kkv), hs_ref[...])
    kkn = kkv / jnp.maximum(jnp.sqrt(ss), 1e-12)
    k_o[...] = _bf(k * (1.0 + (iclr - 1.0) * ka_ref[...]))
    if has_vres:
        vl = pa[:, 3 * RW + 3 * LORA_PAD:RW_COLS]
        v = v + (vfirst_ref[...] - v) * _sigmoid(v0_ref[...] + _dot(_bf(vl), vup_ref[...]))
    r_o[...] = _bf(r)
    v_o[...] = _bf(v)
    a_o[...] = _bf(-kkn)
    b_o[...] = _bf(kkn * iclr)

    zz = 0.5 * pb * (1.0 + lax.erf(pb * (2.0 ** -0.5)))
    u = zz[:, :SGU_W]
    vv = _layer_norm(zz[:, SGU_W:], slng_ref[...], slnb_ref[...])
    prow = lax.broadcasted_iota(jnp.int32, (SGU_CHUNK, SGU_GROUPS * SGU_CHUNK), 0)
    qcol = lax.broadcasted_iota(jnp.int32, (SGU_CHUNK, SGU_GROUPS * SGU_CHUNK), 1) % SGU_CHUNK
    wc = _bf(jnp.where(qcol <= prow, sw_ref[...], 0.0))
    lane_grp = lax.broadcasted_iota(jnp.int32, (SGU_CHUNK, SGU_W), 1) // (SGU_W // SGU_GROUPS)
    mixed = []
    for c in range(ts // SGU_CHUNK):
        vc = vv[c * SGU_CHUNK:(c + 1) * SGU_CHUNK]
        stack = jnp.concatenate([jnp.where(lane_grp == g, vc, 0.0) for g in range(SGU_GROUPS)], axis=0)
        mixed.append(_dot(wc, _bf(stack)) + sbias_ref[...])
    mixed = jnp.concatenate(mixed, axis=0)
    o_b = _dot(_bf(u * mixed), swo_ref[...])

    hext[32:32 + ts, :] = pc[:, :CONV_W] * _sigmoid(pc[:, CONV_W:])
    acc = jnp.zeros((ts, CONV_W), F32) + cdb_ref[...]
    first = 32 - (CONV_K - 1)
    for rem in range(8):
        taps = [j for j in range(CONV_K) if (first + j) % 8 == rem]
        span = max(first + j for j in taps) - rem
        hshift[0:span + ts, :] = hext[pl.ds(rem, span + ts), :]
        for j in taps:
            off = first + j - rem
            acc = acc + cdw_ref[j:j + 1, :] * hshift[off:off + ts, :]
    hext[0:32, :] = hext[ts:ts + 32, :]
    hc = _layer_norm(acc, clng_ref[...], clnb_ref[...])
    hc = hc * _sigmoid(hc)
    o_c = _dot(_bf(hc), cwo_ref[...])

    gates = _sigmoid(_bf(gate_logits))
    ga_o[...] = gates[:, :D_MODEL]
    part_o[...] = _bf(gates[:, D_MODEL:2 * D_MODEL] * o_b + gates[:, 2 * D_MODEL:] * o_c)


def _mix_in(x, p, v_first, bsz, seq):
    t = bsz * seq
    ts = SEQ_TILE
    nst = seq // ts
    has_vres = v_first is not None
    row = lambda c: pl.BlockSpec((ts, c), lambda b, s: (b * nst + s, 0))
    args = [x, p['w_rw'], p['w_sgu'], p['w_conv'], p['w_gate'], p['mu'], p['w0'], p['w_up'], p['a0'], p['a_up'],
            p['g_up'], p['k_k'], p['k_a']]
    specs = [row(D_MODEL)] + [_const_spec(a.shape) for a in args[1:]]
    if has_vres:
        args += [p['v0'], p['v_up'], v_first]
        specs += [_const_spec(p['v0'].shape), _const_spec(p['v_up'].shape), row(RW)]
    tail = [p['head_sum'], p['sgu_ln_g'], p['sgu_ln_b'], p['sgu_w'], p['sgu_bias'], p['sgu_w_o'],
            p['conv_dw'], p['conv_db'], p['conv_ln_g'], p['conv_ln_b'], p['conv_w_o']]
    args += tail
    specs += [_const_spec(a.shape) for a in tail]
    out_shape = ([jax.ShapeDtypeStruct((t, RW), F32 if i == 1 else BF16) for i in range(7)]
                 + [jax.ShapeDtypeStruct((t, D_MODEL), BF16)] * 2)
    out_specs = [row(RW)] * 7 + [row(D_MODEL)] * 2
    return pl.pallas_call(
        functools.partial(_mix_in_body, has_vres, ts),
        out_shape=out_shape,
        grid=(bsz, nst),
        in_specs=specs,
        out_specs=out_specs,
        scratch_shapes=[pltpu.VMEM((ts + 8, RW_COLS), F32), pltpu.VMEM((ts + 32, CONV_W), F32),
                        pltpu.VMEM((ts + 32, CONV_W), F32)],
        compiler_params=pltpu.CompilerParams(dimension_semantics=("arbitrary", "arbitrary"),
                                             vmem_limit_bytes=VMEM_LIMIT),
        name="mix_in",
    )(*args)


def _wkv_body(tb, r_ref, lw_ref, k_ref, v_ref, a_ref, b_ref, y_ref, st_ref, wr_s, rs_s, ar_s, vs_s, kb_s, dc_s):
    c = WKV_CHUNK
    nc = tb // c
    npair = HEADS // 2
    units = [(ci, p) for ci in range(nc) for p in range(npair)]

    @pl.when(pl.program_id(1) == 0)
    def _():
        st_ref[...] = jnp.zeros(st_ref.shape, F32)

    lane = lax.broadcasted_iota(jnp.int32, (c, PAIR), 1)
    head0 = lane < HEAD_DIM
    ti = lax.broadcasted_iota(jnp.int32, (PAIR, PAIR), 0)
    si = lax.broadcasted_iota(jnp.int32, (PAIR, PAIR), 1)
    same = (ti // c) == (si // c)
    lower = same & ((si % c) <= (ti % c))
    strict = same & ((si % c) < (ti % c))
    eye = (ti == si).astype(F32)
    tr = lax.broadcasted_iota(jnp.int32, (tb, tb), 0)
    tc = lax.broadcasted_iota(jnp.int32, (tb, tb), 1)
    tri = _bf(((tr // c == tc // c) & (tc <= tr)).astype(F32))

    def stack(x):
        return jnp.concatenate([jnp.where(head0, x, 0.0), jnp.where(head0, 0.0, x)], axis=0)

    lw = lw_ref[...]
    cum = _split_dot_left(tri, lw)
    cum_end = jnp.concatenate(
        [jnp.broadcast_to(cum[(ci + 1) * c - 1:(ci + 1) * c, :], (c, RW)) for ci in range(nc)], axis=0)
    p_inv = jnp.exp(-cum)
    p_end = jnp.exp(cum_end - cum)
    rt = r_ref[...] * jnp.exp(cum)
    at = a_ref[...] * jnp.exp(cum - lw)
    kt = k_ref[...] * p_inv
    bt = b_ref[...] * p_inv
    ke = k_ref[...] * p_end
    be = b_ref[...] * p_end
    vt = v_ref[...]

    def sub(x, ci, p):
        return x[ci * c:(ci + 1) * c, p * PAIR:(p + 1) * PAIR]

    a_s = [stack(sub(at, ci, p)) for ci, p in units]
    r_s = [stack(sub(rt, ci, p)) for ci, p in units]
    g = [_dot_nt(_bf(jnp.concatenate([r_s[i], a_s[i]], axis=0)),
                 _bf(jnp.concatenate([stack(sub(kt, ci, p)), stack(sub(bt, ci, p))], axis=0)))
         for i, (ci, p) in enumerate(units)]
    for i, (ci, p) in enumerate(units):
        vs_s[ci, p] = _bf(stack(sub(vt, ci, p)))
        ar_s[ci, p] = _bf(jnp.concatenate([jnp.where(lower, g[i][0:PAIR, 0:PAIR], 0.0),
                                           jnp.where(lower, g[i][0:PAIR, PAIR:], 0.0)], axis=1))
        kb_s[ci, p] = _bf(jnp.concatenate([stack(sub(ke, ci, p)).T, stack(sub(be, ci, p)).T], axis=1))
        dc_s[ci, p] = jnp.exp(jnp.broadcast_to(sub(cum_end, ci, p)[0:1, :], (PAIR, PAIR))).T
    apow = [jnp.where(strict, g[i][PAIR:, PAIR:], 0.0) for i in range(len(units))]
    inv = [eye + x for x in apow]
    apb = [_bf(x) for x in apow]
    apow = [_dot(x, x) for x in apb]
    for _ in range(4):
        apb = [_bf(x) for x in apow]
        both = [_dot(x, jnp.concatenate([x, _bf(y)], axis=1)) for x, y in zip(apb, inv)]
        apow = [x[:, :PAIR] for x in both]
        inv = [y + x[:, PAIR:] for x, y in zip(both, inv)]
    inv = [y + _dot(_bf(x), _bf(y)) for x, y in zip(apow, inv)]
    for i, (ci, p) in enumerate(units):
        a_ak = jnp.where(strict, g[i][PAIR:, 0:PAIR], 0.0)
        wr_s[ci, p] = _bf(_dot(_bf(inv[i]), _bf(jnp.concatenate([a_s[i], a_ak], axis=1))))
        rs_s[ci, p] = _bf(r_s[i])

    s = [st_ref[p] for p in range(npair)]
    for ci in range(nc):
        sb = [_bf(x) for x in s]
        u = [_dot(wr_s[ci, p], jnp.concatenate([sb[p], vs_s[ci, p]], axis=0)) for p in range(npair)]
        vu = [jnp.concatenate([vs_s[ci, p], _bf(u[p])], axis=0) for p in range(npair)]
        s = [dc_s[ci, p] * s[p] + _dot(kb_s[ci, p], vu[p]) for p in range(npair)]
        for p in range(npair):
            y = _dot(rs_s[ci, p], sb[p]) + _dot(ar_s[ci, p], vu[p])
            y_ref[ci * c:(ci + 1) * c, p * PAIR:(p + 1) * PAIR] = y[0:c] + y[c:]
    for p in range(npair):
        st_ref[p] = s[p]


def _wkv(r, lw, k, v, a, b, bsz, seq):
    t = bsz * seq
    tb = WKV_TILE
    nst = seq // tb
    nc, npair = tb // WKV_CHUNK, HEADS // 2
    row = pl.BlockSpec((tb, RW), lambda bi, s: (bi * nst + s, 0))
    return pl.pallas_call(
        functools.partial(_wkv_body, tb),
        out_shape=jax.ShapeDtypeStruct((t, RW), F32),
        grid=(bsz, nst),
        in_specs=[row] * 6,
        out_specs=row,
        scratch_shapes=[pltpu.VMEM((npair, PAIR, PAIR), F32),
                        pltpu.VMEM((nc, npair, PAIR, 2 * PAIR), BF16),
                        pltpu.VMEM((nc, npair, PAIR, PAIR), BF16),
                        pltpu.VMEM((nc, npair, PAIR, 2 * PAIR), BF16),
                        pltpu.VMEM((nc, npair, PAIR, PAIR), BF16),
                        pltpu.VMEM((nc, npair, PAIR, 2 * PAIR), BF16),
                        pltpu.VMEM((nc, npair, PAIR, PAIR), F32)],
        compiler_params=pltpu.CompilerParams(dimension_semantics=("arbitrary", "arbitrary"),
                                             vmem_limit_bytes=VMEM_LIMIT),
        name="wkv",
    )(r, lw, k, v, a, b)


def _mix_out_body(halves, y_ref, r_ref, k_ref, v_ref, g_ref, ga_ref, part_ref, x_ref, hs_ref, rk_ref, gng_ref,
                  gnb_ref, wo_ref, wout_ref, ln_g_ref, ln_b_ref, rw_hi_ref, rw_cat_ref, rb_ref, x1_o, x1b_o, route_o,
                  route_t_o, counts_o, cnt_ref):
    th = y_ref.shape[0] // halves
    hrows = [pl.ds(h * th, th) for h in range(halves)]

    def each(fn, *cols):
        return [fn(*a) for a in zip(*cols)]

    def load(ref):
        return [ref[rs, :] for rs in hrows]

    hs = hs_ref[...]
    inv_n = 1.0 / HEAD_DIM
    y = load(y_ref)
    mu = each(lambda a: _head_sum_split(a, hs) * inv_n, y)
    d = each(lambda a, m: a - m, y, mu)
    var = each(lambda a: _head_sum(_bf(a * a), hs) * inv_n, d)
    rkk = each(lambda a, b: _head_sum(_bf(a * b * rk_ref[...]), hs), load(r_ref), load(k_ref))
    yn = each(lambda a, s2: a * lax.rsqrt(s2 + GN_EPS) * gng_ref[...] + gnb_ref[...], d, var)
    out = each(lambda a, bo, vv, gg: _bf((a + bo * vv) * gg), yn, rkk, load(v_ref), load(g_ref))
    o_a = each(lambda a: _dot(a, wo_ref[...]), out)
    merged = each(lambda a, b, c2: _bf(a * b + c2), load(ga_ref), o_a, load(part_ref))
    xm = each(lambda a: _dot(a, wout_ref[...]), merged)
    x1 = each(lambda a, b: _layer_norm(ALPHA * a + b, ln_g_ref[...], ln_b_ref[...]), load(x_ref), xm)
    xh = each(_bf, x1)
    for rs, a, b in zip(hrows, x1, xh):
        x1_o[rs, :] = a
        x1b_o[rs, :] = b

    xl = each(lambda a, b: _bf(a - b.astype(F32)), x1, xh)
    hcat = each(lambda a: _dot(a, rw_cat_ref[...]), xh)
    logits = each(lambda c2, b: c2[:, :128] + c2[:, 128:] + _dot(b, rw_hi_ref[...]), hcat, xl)
    lt = each(lambda a: a.T[0:N_EXPERTS, :], logits)
    scores = each(_sigmoid, lt)
    bias = jnp.concatenate([rb_ref[...]] * (th // 128), axis=1)
    sel = each(lambda a: a + bias, scores)
    row = lax.broadcasted_iota(jnp.int32, (N_EXPERTS, th), 0).astype(F32)
    grp_row = lax.broadcasted_iota(jnp.int32, (GROUP_SIZE, th), 0).astype(F32)
    neg = jnp.float32(-jnp.inf)

    def top2(sg, rg):
        m1 = jnp.max(sg, axis=0, keepdims=True)
        i1 = jnp.min(jnp.where(sg == m1, rg, 1e9), axis=0, keepdims=True)
        sg2 = jnp.where(rg == i1, neg, sg)
        m2 = jnp.max(sg2, axis=0, keepdims=True)
        i2 = jnp.min(jnp.where(sg2 == m2, rg, 1e9), axis=0, keepdims=True)
        return m1 + m2, i1, i2

    best = e0 = e1 = None
    for g in range(N_GROUPS):
        grp = slice(g * GROUP_SIZE, (g + 1) * GROUP_SIZE)
        cand = each(lambda a: top2(a[grp], grp_row + float(g * GROUP_SIZE)), sel)
        if g == 0:
            best, e0, e1 = ([c[i] for c in cand] for i in range(3))
        else:
            take = each(lambda c, b: c[0] > b, cand, best)
            best = each(lambda tk, c, b: jnp.where(tk, c[0], b), take, cand, best)
            e0 = each(lambda tk, c, b: jnp.where(tk, c[1], b), take, cand, e0)
            e1 = each(lambda tk, c, b: jnp.where(tk, c[2], b), take, cand, e1)
    g0 = each(lambda e, sc: jnp.sum(jnp.where(row == e, sc, 0.0), axis=0, keepdims=True), e0, scores)
    g1 = each(lambda e, sc: jnp.sum(jnp.where(row == e, sc, 0.0), axis=0, keepdims=True), e1, scores)

    @pl.when(pl.program_id(0) == 0)
    def _():
        cnt_ref[...] = jnp.zeros(cnt_ref.shape, F32)

    chosen = each(lambda a, b: jnp.where((row == a) | (row == b), 1.0, 0.0), e0, e1)
    earlier = _bf((lax.broadcasted_iota(jnp.int32, (th, th), 0)
                   < lax.broadcasted_iota(jnp.int32, (th, th), 1)).astype(F32))
    inside = each(lambda a: _dot(_bf(a), earlier), chosen)
    total = each(lambda a: jnp.sum(a, axis=1, keepdims=True), chosen)
    base = cnt_ref[...]
    row8 = lax.broadcasted_iota(jnp.int32, (8, th), 0)
    for h in range(halves):
        seen = inside[h] + jnp.concatenate([base] * (th // 128), axis=1)
        rank0 = jnp.sum(jnp.where(row == e0[h], seen, 0.0), axis=0, keepdims=True)
        rank1 = jnp.sum(jnp.where(row == e1[h], seen, 0.0), axis=0, keepdims=True)
        tot = g0[h] + g1[h]
        route_t = jnp.where(row8 == 0, e0[h], jnp.where(row8 == 1, e1[h], jnp.where(
            row8 == 2, g0[h] / tot, jnp.where(row8 == 3, g1[h] / tot, jnp.where(
                row8 == 4, rank0, jnp.where(row8 == 5, rank1, 0.0))))))
        route_t_o[:, hrows[h]] = route_t
        route_o[hrows[h], :] = jnp.concatenate([route_t, jnp.zeros((120, th), F32)], axis=0).T
        base = base + total[h]
    cnt_ref[...] = base
    counts_o[...] = base


def _mix_out(y, r, k, v, g, ga, part, x, p, bsz, seq):
    t = bsz * seq
    ts = MIX_OUT_HALVES * SEQ_TILE
    row = lambda c: pl.BlockSpec((ts, c), lambda i: (i, 0))
    consts = [p['head_sum'], p['r_k'], p['gn_g'], p['gn_b'], p['rwkv_w_o'], p['w_out'], p['ln1_g'], p['ln1_b'],
              p['router_hi'], p['router_cat'], p['router_b']]
    return pl.pallas_call(
        functools.partial(_mix_out_body, MIX_OUT_HALVES),
        out_shape=[jax.ShapeDtypeStruct((t, D_MODEL), F32), jax.ShapeDtypeStruct((t, D_MODEL), BF16),
                   jax.ShapeDtypeStruct((t, 128), F32), jax.ShapeDtypeStruct((8, t), F32),
                   jax.ShapeDtypeStruct((N_EXPERTS, 128), F32)],
        grid=(t // ts,),
        in_specs=[row(RW)] * 5 + [row(D_MODEL)] * 3 + [_const_spec(a.shape) for a in consts],
        out_specs=[row(D_MODEL), row(D_MODEL), row(128), pl.BlockSpec((8, ts), lambda i: (0, i)),
                   pl.BlockSpec((N_EXPERTS, 128), lambda i: (0, 0))],
        scratch_shapes=[pltpu.VMEM((N_EXPERTS, 128), F32)],
        compiler_params=pltpu.CompilerParams(dimension_semantics=("arbitrary",), vmem_limit_bytes=VMEM_LIMIT),
        name="mix_out",
    )(y, r, k, v, g, ga, part, x, *consts)


def _experts_body(be_ref, nb_ref, x_ref, wg_ref, wu_ref, wd_ref, o_ref, wg_s, wu_s, wd_s):
    i = pl.program_id(0)
    used = i < nb_ref[0]

    @pl.when(used & ((i == 0) | (be_ref[i] != be_ref[jnp.maximum(i - 1, 0)])))
    def _():
        wg_s[...] = _bf(wg_ref[...])
        wu_s[...] = _bf(wu_ref[...])
        wd_s[...] = _bf(wd_ref[...])

    @pl.when(used)
    def _():
        hb = EXPERT_BLOCK // 2
        rows = [pl.ds(h * hb, hb) for h in range(2)]
        xs = [x_ref[rs, :] for rs in rows]
        hg = [_dot(a, wg_s[...]) for a in xs]
        hu = [_dot(a, wu_s[...]) for a in xs]
        hh = [_bf(a * _sigmoid(a) * b) for a, b in zip(hg, hu)]
        out = [_dot(a, wd_s[...]) for a in hh]
        for rs, a in zip(rows, out):
            o_ref[rs, :] = _bf(a)

    @pl.when(jnp.logical_not(used))
    def _():
        o_ref[...] = jnp.zeros(o_ref.shape, BF16)


def _experts(xg, block_expert, n_used, wg, wu, wd, layer):
    rows = xg.shape[0]
    nb = rows // EXPERT_BLOCK
    grid_spec = pltpu.PrefetchScalarGridSpec(
        num_scalar_prefetch=2,
        grid=(nb,),
        in_specs=[
            pl.BlockSpec((EXPERT_BLOCK, D_MODEL), lambda i, be, nu: (i, 0)),
            pl.BlockSpec((None, None, D_MODEL, D_EXPERT), lambda i, be, nu: (layer, be[i], 0, 0)),
            pl.BlockSpec((None, None, D_MODEL, D_EXPERT), lambda i, be, nu: (layer, be[i], 0, 0)),
            pl.BlockSpec((None, None, D_EXPERT, D_MODEL), lambda i, be, nu: (layer, be[i], 0, 0)),
        ],
        out_specs=pl.BlockSpec((EXPERT_BLOCK, D_MODEL), lambda i, be, nu: (i, 0)),
        scratch_shapes=[pltpu.VMEM((D_MODEL, D_EXPERT), BF16), pltpu.VMEM((D_MODEL, D_EXPERT), BF16),
                        pltpu.VMEM((D_EXPERT, D_MODEL), BF16)],
    )
    return pl.pallas_call(
        _experts_body,
        out_shape=jax.ShapeDtypeStruct((rows, D_MODEL), BF16),
        grid_spec=grid_spec,
        compiler_params=pltpu.CompilerParams(dimension_semantics=("arbitrary",), vmem_limit_bytes=VMEM_LIMIT),
        name="experts",
    )(block_expert, n_used, xg, wg, wu, wd)


def _combine_body(x_ref, y0_ref, y1_ref, route_ref, g_ref, b_ref, o_ref):
    route = route_ref[...]
    moe = route[:, 2:3] * y0_ref[...] + route[:, 3:4] * y1_ref[...]
    o_ref[...] = _layer_norm(ALPHA * x_ref[...] + moe, g_ref[...], b_ref[...])


def _combine(x1, y0, y1, route, ln_g, ln_b):
    t = x1.shape[0]
    ts = COMBINE_TILE
    row = lambda c: pl.BlockSpec((ts, c), lambda i: (i, 0))
    return pl.pallas_call(
        _combine_body,
        out_shape=jax.ShapeDtypeStruct((t, D_MODEL), F32),
        grid=(t // ts,),
        in_specs=[row(D_MODEL)] * 3 + [row(128), _const_spec(ln_g.shape), _const_spec(ln_b.shape)],
        out_specs=row(D_MODEL),
        compiler_params=pltpu.CompilerParams(dimension_semantics=("arbitrary",), vmem_limit_bytes=VMEM_LIMIT),
        name="combine",
    )(x1, y0, y1, route, ln_g, ln_b)


def _dispatch(route_t, counts, t):
    expert = route_t[0:2].astype(jnp.int32)
    rank = route_t[4:6].astype(jnp.int32)
    counts = counts[:, 0].astype(jnp.int32)
    padded = (counts + EXPERT_BLOCK - 1) // EXPERT_BLOCK * EXPERT_BLOCK
    pad_end = jnp.cumsum(padded)
    pad_start = pad_end - padded
    dest = rank
    for e in range(N_EXPERTS):
        dest = dest + jnp.where(expert == e, pad_start[e], 0)
    n_blocks = (2 * t + EXPERT_BLOCK - 1) // EXPERT_BLOCK + N_EXPERTS
    block_start = jnp.arange(n_blocks, dtype=jnp.int32) * EXPERT_BLOCK
    block_expert = jnp.minimum(jnp.sum((pad_end[None, :] <= block_start[:, None]).astype(jnp.int32), axis=1),
                               N_EXPERTS - 1)
    n_used = (pad_end[-1:] // EXPERT_BLOCK).astype(jnp.int32)
    by_row = jnp.argsort(dest.reshape(2 * t)).astype(jnp.int32) % t
    pad_before = jnp.cumsum(padded - counts) - (padded - counts)
    src = jnp.arange(n_blocks * EXPERT_BLOCK, dtype=jnp.int32) - jnp.repeat(pad_before[block_expert], EXPERT_BLOCK)
    buf_tok = jnp.take(by_row, src % (2 * t), mode='clip')
    return dest, buf_tok, block_expert, n_used


def _pad_cols(w, width=LORA_PAD):
    return jnp.pad(w, ((0, 0), (0, width - w.shape[1])))


def _pad_rows(w, height=LORA_PAD):
    return jnp.pad(w, ((0, height - w.shape[0]), (0, 0)))


def _row(v):
    return v.reshape(1, -1).astype(F32)


def _layer_params(l, a):
    w_in = a['w_in_first'] if l == 0 else a['w_in_rest'][l - 1]
    if l == 0:
        w_vres = jnp.zeros((D_MODEL, LORA_PAD), F32)
        mu_vres = jnp.zeros((LORA_PAD,), F32)
    else:
        w_vres = _pad_cols(w_in[:, _O_VRES:])
        mu_vres = jnp.pad(a['rwkv_mu_vres'][l - 1], (0, LORA_PAD - 32))
    mu = a['rwkv_mu'][l]
    p = {
        'w_rw': _bf(jnp.concatenate([w_in[:, :_O_WL], _pad_cols(w_in[:, _O_WL:_O_AL]), _pad_cols(w_in[:, _O_AL:_O_GL]),
                                     _pad_cols(w_in[:, _O_GL:_O_SGU]), w_vres], axis=1)),
        'w_sgu': _bf(w_in[:, _O_SGU:_O_CONV]),
        'w_conv': _bf(w_in[:, _O_CONV:_O_GATE]),
        'w_gate': _bf(w_in[:, _O_GATE:_O_VRES]),
        'mu': _row(jnp.concatenate([mu[:_O_WL], jnp.pad(mu[_O_WL:_O_AL], (0, 96)), jnp.pad(mu[_O_AL:_O_GL], (0, 96)),
                                    jnp.pad(mu[_O_GL:], (0, 32)), mu_vres])),
        'w0': _row(a['rwkv_w0'][l]),
        'w_up': _bf(_pad_rows(a['rwkv_w_up'][l])),
        'a0': _row(a['rwkv_a0'][l]),
        'a_up': _bf(_pad_rows(a['rwkv_a_up'][l])),
        'g_up': _bf(_pad_rows(a['rwkv_g_up'][l])),
        'k_k': _row(a['rwkv_k_k'][l]),
        'k_a': _row(a['rwkv_k_a'][l]),
        'r_k': _row(a['rwkv_r_k'][l]),
        'gn_g': _row(a['rwkv_gn_g'][l]),
        'gn_b': _row(a['rwkv_gn_b'][l]),
        'rwkv_w_o': _bf(a['rwkv_w_o'][l]),
        'sgu_ln_g': _row(a['sgu_ln_g'][l]),
        'sgu_ln_b': _row(a['sgu_ln_b'][l]),
        'sgu_w': jnp.transpose(a['sgu_w'][l], (1, 0, 2)).reshape(SGU_CHUNK, SGU_GROUPS * SGU_CHUNK),
        'sgu_bias': jnp.repeat(jnp.transpose(a['sgu_b'][l]), SGU_W // SGU_GROUPS, axis=1),
        'sgu_w_o': _bf(a['sgu_w_o'][l]),
        'conv_dw': jnp.pad(a['conv_dw'][l], ((0, 1), (0, 0))),
        'conv_db': _row(a['conv_db'][l]),
        'conv_ln_g': _row(a['conv_ln_g'][l]),
        'conv_ln_b': _row(a['conv_ln_b'][l]),
        'conv_w_o': _bf(a['conv_w_o'][l]),
        'w_out': _bf(a['w_out'][l]),
        'ln1_g': _row(a['ln1_g'][l]),
        'ln1_b': _row(a['ln1_b'][l]),
        'ln2_g': _row(a['ln2_g'][l]),
        'ln2_b': _row(a['ln2_b'][l]),
    }
    if l > 0:
        p['v0'] = _row(a['rwkv_v0'][l - 1])
        p['v_up'] = _bf(_pad_rows(a['rwkv_v_up'][l - 1]))
    return p


def _forward(a):
    x = a['x']
    bsz, seq, _ = x.shape
    t = bsz * seq
    x = x.reshape(t, D_MODEL)
    head = jnp.arange(HEAD_SLAB, dtype=jnp.int32) // HEAD_DIM
    head_sum = _bf((head[:, None] == head[None, :]).astype(F32))
    rw = _pad_cols(a['router_w'])
    rw_hi = _bf(rw)
    rw_lo = _bf(rw - rw_hi.astype(F32))
    rw_cat = jnp.concatenate([rw_hi, rw_lo], axis=1)
    rb = jnp.broadcast_to(a['router_b'].astype(F32)[:, None], (N_EXPERTS, 128))
    v_first = None
    for l in range(DEPTH):
        p = _layer_params(l, a)
        p.update(head_sum=head_sum, router_hi=rw_hi, router_cat=rw_cat, router_b=rb)
        r, lw, k, v, na, nb, g, ga, part = _mix_in(x, p, v_first, bsz, seq)
        if l == 0:
            v_first = v
        y = _wkv(r, lw, k, v, na, nb, bsz, seq)
        x1, x1b, route, route_t, counts = _mix_out(y, r, k, v, g, ga, part, x, p, bsz, seq)
        dest, buf_tok, block_expert, n_used = _dispatch(route_t, counts, t)
        xg = jnp.take(x1b, buf_tok, axis=0, mode='clip')
        yb = _experts(xg, block_expert, n_used, a['moe_w_gate'], a['moe_w_up'], a['moe_w_down'], l)
        y0 = jnp.take(yb, dest[0], axis=0, mode='clip')
        y1 = jnp.take(yb, dest[1], axis=0, mode='clip')
        x = _combine(x1, y0, y1, route, p['ln2_g'], p['ln2_b'])
    return x.reshape(bsz, seq, D_MODEL)


def kernel(x, w_in_first, w_in_rest, rwkv_mu, rwkv_mu_vres, rwkv_w0, rwkv_w_up, rwkv_a0, rwkv_a_up, rwkv_g_up, rwkv_v0, rwkv_v_up, rwkv_k_k, rwkv_k_a, rwkv_r_k, rwkv_gn_g, rwkv_gn_b, rwkv_w_o, sgu_ln_g, sgu_ln_b, sgu_w, sgu_b, sgu_w_o, conv_dw, conv_db, conv_ln_g, conv_ln_b, conv_w_o, w_out, ln1_g, ln1_b, router_w, router_b, moe_w_gate, moe_w_up, moe_w_down, ln2_g, ln2_b):
    return _forward(dict(
        x=x, w_in_first=w_in_first, w_in_rest=w_in_rest, rwkv_mu=rwkv_mu, rwkv_mu_vres=rwkv_mu_vres,
        rwkv_w0=rwkv_w0, rwkv_w_up=rwkv_w_up, rwkv_a0=rwkv_a0, rwkv_a_up=rwkv_a_up, rwkv_g_up=rwkv_g_up,
        rwkv_v0=rwkv_v0, rwkv_v_up=rwkv_v_up, rwkv_k_k=rwkv_k_k, rwkv_k_a=rwkv_k_a, rwkv_r_k=rwkv_r_k,
        rwkv_gn_g=rwkv_gn_g, rwkv_gn_b=rwkv_gn_b, rwkv_w_o=rwkv_w_o, sgu_ln_g=sgu_ln_g, sgu_ln_b=sgu_ln_b,
        sgu_w=sgu_w, sgu_b=sgu_b, sgu_w_o=sgu_w_o, conv_dw=conv_dw, conv_db=conv_db, conv_ln_g=conv_ln_g,
        conv_ln_b=conv_ln_b, conv_w_o=conv_w_o, w_out=w_out, ln1_g=ln1_g, ln1_b=ln1_b, router_w=router_w,
        router_b=router_b, moe_w_gate=moe_w_gate, moe_w_up=moe_w_up, moe_w_down=moe_w_down, ln2_g=ln2_g,
        ln2_b=ln2_b))
```

```python
import functools

import jax
import jax.numpy as jnp
from jax import lax
from jax.experimental import pallas as pl
from jax.experimental.pallas import tpu as pltpu

F32 = jnp.float32
BF16 = jnp.bfloat16

D_MODEL = 1024
DEPTH = 4
HEADS = 8
HEAD_DIM = 64
RW = HEADS * HEAD_DIM
LORA_PAD = 128
RW_COLS = 3 * RW + 4 * LORA_PAD
SGU_W = 256
SGU_CHUNK = 128
SGU_GROUPS = 4
CONV_W = 256
CONV_K = 31
N_EXPERTS = 32
GROUP_SIZE = 8
N_GROUPS = 4
D_EXPERT = 512
EXPERT_BLOCK = 512
ALPHA = (2 * DEPTH) ** 0.25
LN_EPS = 1e-5
GN_EPS = 64e-5

_O_WL = 3 * RW
_O_AL = _O_WL + 32
_O_GL = _O_AL + 32
_O_SGU = _O_GL + 96
_O_CONV = _O_SGU + 2 * SGU_W
_O_GATE = _O_CONV + 2 * CONV_W
_O_VRES = _O_GATE + 3 * D_MODEL

SEQ_TILE = 256
WKV_CHUNK = 64
WKV_TILE = 256
MIX_OUT_HALVES = 2
COMBINE_TILE = 512
STREAMS = 2
PAIR = 2 * HEAD_DIM
HEAD_SLAB = 256
VMEM_LIMIT = 48 * 1024 * 1024


def _dot(a, b):
    return jnp.dot(a, b, preferred_element_type=F32)


def _dot_nt(a, b):
    return lax.dot_general(a, b, (((1,), (1,)), ((), ())), preferred_element_type=F32)


def _bf(x):
    return x.astype(BF16)


def _head_sum(x_bf16, ones_bd):
    w = ones_bd.shape[0]
    return jnp.concatenate([_dot(x_bf16[:, i:i + w], ones_bd) for i in range(0, x_bf16.shape[1], w)], axis=1)


def _head_sum_split(x, ones_bd):
    hi = _bf(x)
    lo = _bf(x - hi.astype(F32))
    return _head_sum(hi, ones_bd) + _head_sum(lo, ones_bd)


def _split_dot_left(w_bf16, x):
    hi = _bf(x)
    lo = _bf(x - hi.astype(F32))
    return _dot(w_bf16, hi) + _dot(w_bf16, lo)


def _layer_norm(x, g, b):
    mu = jnp.mean(x, axis=-1, keepdims=True)
    d = x - mu
    var = jnp.mean(d * d, axis=-1, keepdims=True)
    return d * lax.rsqrt(var + LN_EPS) * g + b


def _sigmoid(x):
    return 1.0 / (1.0 + jnp.exp(-x))


def _const_spec(shape):
    nd = len(shape)
    return pl.BlockSpec(shape, lambda *_: (0,) * nd, pipeline_mode=pl.Buffered(1))


def _mix_in_body(has_vres, ts, *refs):
    it = iter(refs)
    x_ref, wrw_ref, wsgu_ref, wconv_ref, wgate_ref = (next(it) for _ in range(5))
    mu_ref, w0_ref, wup_ref, a0_ref, aup_ref, gup_ref, kk_ref, ka_ref = (next(it) for _ in range(8))
    if has_vres:
        v0_ref, vup_ref, vfirst_ref = (next(it) for _ in range(3))
    hs_ref = next(it)
    slng_ref, slnb_ref, sw_ref, sbias_ref, swo_ref = (next(it) for _ in range(5))
    cdw_ref, cdb_ref, clng_ref, clnb_ref, cwo_ref = (next(it) for _ in range(5))
    r_o, lw_o, k_o, v_o, a_o, b_o, g_o, ga_o, part_o = (next(it) for _ in range(9))
    pbuf, hext, hshift = next(it), next(it), next(it)

    @pl.when(pl.program_id(1) == 0)
    def _():
        pbuf[0:8, :] = jnp.zeros((8, RW_COLS), F32)
        hext[0:32, :] = jnp.zeros((32, CONV_W), F32)

    xb = _bf(x_ref[...])
    pr = _dot(xb, wrw_ref[...])
    pb = _dot(xb, wsgu_ref[...])
    pc = _dot(xb, wconv_ref[...])
    gate_logits = _dot(xb, wgate_ref[...])

    pbuf[8:8 + ts, :] = pr
    prev = pbuf[pl.ds(7, ts), :]
    pa = pr + (prev - pr) * mu_ref[...]
    pbuf[0:8, :] = pbuf[ts:ts + 8, :]
    r = pa[:, 0:RW]
    k = pa[:, RW:2 * RW]
    v = pa[:, 2 * RW:3 * RW]
    wl = pa[:, 3 * RW:3 * RW + LORA_PAD]
    al = pa[:, 3 * RW + LORA_PAD:3 * RW + 2 * LORA_PAD]
    gl = pa[:, 3 * RW + 2 * LORA_PAD:3 * RW + 3 * LORA_PAD]
    z = w0_ref[...] + _dot(_bf(jnp.tanh(wl)), wup_ref[...])
    nz = -z
    softplus = jnp.maximum(nz, 0.0) + jnp.log(1.0 + jnp.exp(-jnp.abs(nz)))
    lw_o[...] = -jnp.exp(-softplus - 0.5)
    iclr = _sigmoid(a0_ref[...] + _dot(_bf(al), aup_ref[...]))
    g_o[...] = _bf(_dot(_bf(_sigmoid(gl)), gup_ref[...]))
    kkv = k * kk_ref[...]
    ss = _head_sum(_bf(kkv * kkv), hs_ref[...])
    kkn = kkv / jnp.maximum(jnp.sqrt(ss), 1e-12)
    k_o[...] = _bf(k * (1.0 + (iclr - 1.0) * ka_ref[...]))
    if has_vres:
        vl = pa[:, 3 * RW + 3 * LORA_PAD:RW_COLS]
        v = v + (vfirst_ref[...] - v) * _sigmoid(v0_ref[...] + _dot(_bf(vl), vup_ref[...]))
    r_o[...] = _bf(r)
    v_o[...] = _bf(v)
    a_o[...] = _bf(-kkn)
    b_o[...] = _bf(kkn * iclr)

    zz = 0.5 * pb * (1.0 + lax.erf(pb * (2.0 ** -0.5)))
    u = zz[:, :SGU_W]
    vv = _layer_norm(zz[:, SGU_W:], slng_ref[...], slnb_ref[...])
    prow = lax.broadcasted_iota(jnp.int32, (SGU_CHUNK, SGU_GROUPS * SGU_CHUNK), 0)
    qcol = lax.broadcasted_iota(jnp.int32, (SGU_CHUNK, SGU_GROUPS * SGU_CHUNK), 1) % SGU_CHUNK
    wc = _bf(jnp.where(qcol <= prow, sw_ref[...], 0.0))
    lane_grp = lax.broadcasted_iota(jnp.int32, (SGU_CHUNK, SGU_W), 1) // (SGU_W // SGU_GROUPS)
    mixed = []
    for c in range(ts // SGU_CHUNK):
        vc = vv[c * SGU_CHUNK:(c + 1) * SGU_CHUNK]
        stack = jnp.concatenate([jnp.where(lane_grp == g, vc, 0.0) for g in range(SGU_GROUPS)], axis=0)
        mixed.append(_dot(wc, _bf(stack)) + sbias_ref[...])
    mixed = jnp.concatenate(mixed, axis=0)
    o_b = _dot(_bf(u * mixed), swo_ref[...])

    hext[32:32 + ts, :] = pc[:, :CONV_W] * _sigmoid(pc[:, CONV_W:])
    acc = jnp.zeros((ts, CONV_W), F32) + cdb_ref[...]
    first = 32 - (CONV_K - 1)
    for rem in range(8):
        taps = [j for j in range(CONV_K) if (first + j) % 8 == rem]
        span = max(first + j for j in taps) - rem
        hshift[0:span + ts, :] = hext[pl.ds(rem, span + ts), :]
        for j in taps:
            off = first + j - rem
            acc = acc + cdw_ref[j:j + 1, :] * hshift[off:off + ts, :]
    hext[0:32, :] = hext[ts:ts + 32, :]
    hc = _layer_norm(acc, clng_ref[...], clnb_ref[...])
    hc = hc * _sigmoid(hc)
    o_c = _dot(_bf(hc), cwo_ref[...])

    gates = _sigmoid(_bf(gate_logits))
    ga_o[...] = gates[:, :D_MODEL]
    part_o[...] = _bf(gates[:, D_MODEL:2 * D_MODEL] * o_b + gates[:, 2 * D_MODEL:] * o_c)


def _mix_in(x, p, v_first, bsz, seq):
    t = bsz * seq
    ts = SEQ_TILE
    nst = seq // ts
    has_vres = v_first is not None
    row = lambda c: pl.BlockSpec((ts, c), lambda b, s: (b * nst + s, 0))
    args = [x, p['w_rw'], p['w_sgu'], p['w_conv'], p['w_gate'], p['mu'], p['w0'], p['w_up'], p['a0'], p['a_up'],
            p['g_up'], p['k_k'], p['k_a']]
    specs = [row(D_MODEL)] + [_const_spec(a.shape) for a in args[1:]]
    if has_vres:
        args += [p['v0'], p['v_up'], v_first]
        specs += [_const_spec(p['v0'].shape), _const_spec(p['v_up'].shape), row(RW)]
    tail = [p['head_sum'], p['sgu_ln_g'], p['sgu_ln_b'], p['sgu_w'], p['sgu_bias'], p['sgu_w_o'],
            p['conv_dw'], p['conv_db'], p['conv_ln_g'], p['conv_ln_b'], p['conv_w_o']]
    args += tail
    specs += [_const_spec(a.shape) for a in tail]
    out_shape = ([jax.ShapeDtypeStruct((t, RW), F32 if i == 1 else BF16) for i in range(7)]
                 + [jax.ShapeDtypeStruct((t, D_MODEL), BF16)] * 2)
    out_specs = [row(RW)] * 7 + [row(D_MODEL)] * 2
    return pl.pallas_call(
        functools.partial(_mix_in_body, has_vres, ts),
        out_shape=out_shape,
        grid=(bsz, nst),
        in_specs=specs,
        out_specs=out_specs,
        scratch_shapes=[pltpu.VMEM((ts + 8, RW_COLS), F32), pltpu.VMEM((ts + 32, CONV_W), F32),
                        pltpu.VMEM((ts + 32, CONV_W), F32)],
        compiler_params=pltpu.CompilerParams(dimension_semantics=("arbitrary", "arbitrary"),
                                             vmem_limit_bytes=VMEM_LIMIT),
        name="mix_in",
    )(*args)


def _wkv_body(tb, r_ref, lw_ref, k_ref, v_ref, a_ref, b_ref, y_ref, st_ref, wr_s, rs_s, ar_s, vs_s, kb_s, dc_s):
    c = WKV_CHUNK
    nc = tb // c
    npair = HEADS // 2
    units = [(ci, p) for ci in range(nc) for p in range(npair)]

    @pl.when(pl.program_id(1) == 0)
    def _():
        st_ref[...] = jnp.zeros(st_ref.shape, F32)

    lane = lax.broadcasted_iota(jnp.int32, (c, PAIR), 1)
    head0 = lane < HEAD_DIM
    ti = lax.broadcasted_iota(jnp.int32, (PAIR, PAIR), 0)
    si = lax.broadcasted_iota(jnp.int32, (PAIR, PAIR), 1)
    same = (ti // c) == (si // c)
    lower = same & ((si % c) <= (ti % c))
    strict = same & ((si % c) < (ti % c))
    eye = (ti == si).astype(F32)
    tr = lax.broadcasted_iota(jnp.int32, (tb, tb), 0)
    tc = lax.broadcasted_iota(jnp.int32, (tb, tb), 1)
    tri = _bf(((tr // c == tc // c) & (tc <= tr)).astype(F32))

    def stack(x):
        return jnp.concatenate([jnp.where(head0, x, 0.0), jnp.where(head0, 0.0, x)], axis=0)

    lw = lw_ref[...]
    cum = _split_dot_left(tri, lw)
    cum_end = jnp.concatenate(
        [jnp.broadcast_to(cum[(ci + 1) * c - 1:(ci + 1) * c, :], (c, RW)) for ci in range(nc)], axis=0)
    p_inv = jnp.exp(-cum)
    p_end = jnp.exp(cum_end - cum)
    rt = r_ref[...] * jnp.exp(cum)
    at = a_ref[...] * jnp.exp(cum - lw)
    kt = k_ref[...] * p_inv
    bt = b_ref[...] * p_inv
    ke = k_ref[...] * p_end
    be = b_ref[...] * p_end
    vt = v_ref[...]

    def sub(x, ci, p):
        return x[ci * c:(ci + 1) * c, p * PAIR:(p + 1) * PAIR]

    a_s = [stack(sub(at, ci, p)) for ci, p in units]
    r_s = [stack(sub(rt, ci, p)) for ci, p in units]
    g = [_dot_nt(_bf(jnp.concatenate([r_s[i], a_s[i]], axis=0)),
                 _bf(jnp.concatenate([stack(sub(kt, ci, p)), stack(sub(bt, ci, p))], axis=0)))
         for i, (ci, p) in enumerate(units)]
    for i, (ci, p) in enumerate(units):
        vs_s[ci, p] = _bf(stack(sub(vt, ci, p)))
        ar_s[ci, p] = _bf(jnp.concatenate([jnp.where(lower, g[i][0:PAIR, 0:PAIR], 0.0),
                                           jnp.where(lower, g[i][0:PAIR, PAIR:], 0.0)], axis=1))
        kb_s[ci, p] = _bf(jnp.concatenate([stack(sub(ke, ci, p)).T, stack(sub(be, ci, p)).T], axis=1))
        dc_s[ci, p] = jnp.exp(jnp.broadcast_to(sub(cum_end, ci, p)[0:1, :], (PAIR, PAIR))).T
    apow = [jnp.where(strict, g[i][PAIR:, PAIR:], 0.0) for i in range(len(units))]
    inv = [eye + x for x in apow]
    apb = [_bf(x) for x in apow]
    apow = [_dot(x, x) for x in apb]
    for _ in range(4):
        apb = [_bf(x) for x in apow]
        both = [_dot(x, jnp.concatenate([x, _bf(y)], axis=1)) for x, y in zip(apb, inv)]
        apow = [x[:, :PAIR] for x in both]
        inv = [y + x[:, PAIR:] for x, y in zip(both, inv)]
    inv = [y + _dot(_bf(x), _bf(y)) for x, y in zip(apow, inv)]
    for i, (ci, p) in enumerate(units):
        a_ak = jnp.where(strict, g[i][PAIR:, 0:PAIR], 0.0)
        wr_s[ci, p] = _bf(_dot(_bf(inv[i]), _bf(jnp.concatenate([a_s[i], a_ak], axis=1))))
        rs_s[ci, p] = _bf(r_s[i])

    s = [st_ref[p] for p in range(npair)]
    for ci in range(nc):
        sb = [_bf(x) for x in s]
        u = [_dot(wr_s[ci, p], jnp.concatenate([sb[p], vs_s[ci, p]], axis=0)) for p in range(npair)]
        vu = [jnp.concatenate([vs_s[ci, p], _bf(u[p])], axis=0) for p in range(npair)]
        s = [dc_s[ci, p] * s[p] + _dot(kb_s[ci, p], vu[p]) for p in range(npair)]
        for p in range(npair):
            y = _dot(rs_s[ci, p], sb[p]) + _dot(ar_s[ci, p], vu[p])
            y_ref[ci * c:(ci + 1) * c, p * PAIR:(p + 1) * PAIR] = y[0:c] + y[c:]
    for p in range(npair):
        st_ref[p] = s[p]


def _wkv(r, lw, k, v, a, b, bsz, seq):
    t = bsz * seq
    tb = WKV_TILE
    nst = seq // tb
    nc, npair = tb // WKV_CHUNK, HEADS // 2
    row = pl.BlockSpec((tb, RW), lambda bi, s: (bi * nst + s, 0))
    return pl.pallas_call(
        functools.partial(_wkv_body, tb),
        out_shape=jax.ShapeDtypeStruct((t, RW), F32),
        grid=(bsz, nst),
        in_specs=[row] * 6,
        out_specs=row,
        scratch_shapes=[pltpu.VMEM((npair, PAIR, PAIR), F32),
                        pltpu.VMEM((nc, npair, PAIR, 2 * PAIR), BF16),
                        pltpu.VMEM((nc, npair, PAIR, PAIR), BF16),
                        pltpu.VMEM((nc, npair, PAIR, 2 * PAIR), BF16),
                        pltpu.VMEM((nc, npair, PAIR, PAIR), BF16),
                        pltpu.VMEM((nc, npair, PAIR, 2 * PAIR), BF16),
                        pltpu.VMEM((nc, npair, PAIR, PAIR), F32)],
        compiler_params=pltpu.CompilerParams(dimension_semantics=("arbitrary", "arbitrary"),
                                             vmem_limit_bytes=VMEM_LIMIT),
        name="wkv",
    )(r, lw, k, v, a, b)


def _mix_out_body(halves, y_ref, r_ref, k_ref, v_ref, g_ref, ga_ref, part_ref, x_ref, hs_ref, rk_ref, gng_ref,
                  gnb_ref, wo_ref, wout_ref, ln_g_ref, ln_b_ref, rw_hi_ref, rw_cat_ref, rb_ref, x1_o, x1b_o, route_o,
                  route_t_o, counts_o, cnt_ref):
    th = y_ref.shape[0] // halves
    hrows = [pl.ds(h * th, th) for h in range(halves)]

    def each(fn, *cols):
        return [fn(*a) for a in zip(*cols)]

    def load(ref):
        return [ref[rs, :] for rs in hrows]

    hs = hs_ref[...]
    inv_n = 1.0 / HEAD_DIM
    y = load(y_ref)
    mu = each(lambda a: _head_sum_split(a, hs) * inv_n, y)
    d = each(lambda a, m: a - m, y, mu)
    var = each(lambda a: _head_sum(_bf(a * a), hs) * inv_n, d)
    rkk = each(lambda a, b: _head_sum(_bf(a * b * rk_ref[...]), hs), load(r_ref), load(k_ref))
    yn = each(lambda a, s2: a * lax.rsqrt(s2 + GN_EPS) * gng_ref[...] + gnb_ref[...], d, var)
    out = each(lambda a, bo, vv, gg: _bf((a + bo * vv) * gg), yn, rkk, load(v_ref), load(g_ref))
    o_a = each(lambda a: _dot(a, wo_ref[...]), out)
    merged = each(lambda a, b, c2: _bf(a * b + c2), load(ga_ref), o_a, load(part_ref))
    xm = each(lambda a: _dot(a, wout_ref[...]), merged)
    x1 = each(lambda a, b: _layer_norm(ALPHA * a + b, ln_g_ref[...], ln_b_ref[...]), load(x_ref), xm)
    xh = each(_bf, x1)
    for rs, a, b in zip(hrows, x1, xh):
        x1_o[rs, :] = a
        x1b_o[rs, :] = b

    xl = each(lambda a, b: _bf(a - b.astype(F32)), x1, xh)
    hcat = each(lambda a: _dot(a, rw_cat_ref[...]), xh)
    logits = each(lambda c2, b: c2[:, :128] + c2[:, 128:] + _dot(b, rw_hi_ref[...]), hcat, xl)
    lt = each(lambda a: a.T[0:N_EXPERTS, :], logits)
    scores = each(_sigmoid, lt)
    bias = jnp.concatenate([rb_ref[...]] * (th // 128), axis=1)
    sel = each(lambda a: a + bias, scores)
    row = lax.broadcasted_iota(jnp.int32, (N_EXPERTS, th), 0).astype(F32)
    grp_row = lax.broadcasted_iota(jnp.int32, (GROUP_SIZE, th), 0).astype(F32)
    neg = jnp.float32(-jnp.inf)

    def top2(sg, rg):
        m1 = jnp.max(sg, axis=0, keepdims=True)
        i1 = jnp.min(jnp.where(sg == m1, rg, 1e9), axis=0, keepdims=True)
        sg2 = jnp.where(rg == i1, neg, sg)
        m2 = jnp.max(sg2, axis=0, keepdims=True)
        i2 = jnp.min(jnp.where(sg2 == m2, rg, 1e9), axis=0, keepdims=True)
        return m1 + m2, i1, i2

    best = e0 = e1 = None
    for g in range(N_GROUPS):
        grp = slice(g * GROUP_SIZE, (g + 1) * GROUP_SIZE)
        cand = each(lambda a: top2(a[grp], grp_row + float(g * GROUP_SIZE)), sel)
        if g == 0:
            best, e0, e1 = ([c[i] for c in cand] for i in range(3))
        else:
            take = each(lambda c, b: c[0] > b, cand, best)
            best = each(lambda tk, c, b: jnp.where(tk, c[0], b), take, cand, best)
            e0 = each(lambda tk, c, b: jnp.where(tk, c[1], b), take, cand, e0)
            e1 = each(lambda tk, c, b: jnp.where(tk, c[2], b), take, cand, e1)
    g0 = each(lambda e, sc: jnp.sum(jnp.where(row == e, sc, 0.0), axis=0, keepdims=True), e0, scores)
    g1 = each(lambda e, sc: jnp.sum(jnp.where(row == e, sc, 0.0), axis=0, keepdims=True), e1, scores)

    @pl.when(pl.program_id(0) == 0)
    def _():
        cnt_ref[...] = jnp.zeros(cnt_ref.shape, F32)

    chosen = each(lambda a, b: jnp.where((row == a) | (row == b), 1.0, 0.0), e0, e1)
    earlier = _bf((lax.broadcasted_iota(jnp.int32, (th, th), 0)
                   < lax.broadcasted_iota(jnp.int32, (th, th), 1)).astype(F32))
    inside = each(lambda a: _dot(_bf(a), earlier), chosen)
    total = each(lambda a: jnp.sum(a, axis=1, keepdims=True), chosen)
    base = cnt_ref[...]
    row8 = lax.broadcasted_iota(jnp.int32, (8, th), 0)
    for h in range(halves):
        seen = inside[h] + jnp.concatenate([base] * (th // 128), axis=1)
        rank0 = jnp.sum(jnp.where(row == e0[h], seen, 0.0), axis=0, keepdims=True)
        rank1 = jnp.sum(jnp.where(row == e1[h], seen, 0.0), axis=0, keepdims=True)
        tot = g0[h] + g1[h]
        route_t = jnp.where(row8 == 0, e0[h], jnp.where(row8 == 1, e1[h], jnp.where(
            row8 == 2, g0[h] / tot, jnp.where(row8 == 3, g1[h] / tot, jnp.where(
                row8 == 4, rank0, jnp.where(row8 == 5, rank1, 0.0))))))
        route_t_o[:, hrows[h]] = route_t
        route_o[hrows[h], :] = jnp.concatenate([route_t, jnp.zeros((120, th), F32)], axis=0).T
        base = base + total[h]
    cnt_ref[...] = base
    counts_o[...] = base


def _mix_out(y, r, k, v, g, ga, part, x, p, bsz, seq):
    t = bsz * seq
    ts = MIX_OUT_HALVES * SEQ_TILE
    row = lambda c: pl.BlockSpec((ts, c), lambda i: (i, 0))
    consts = [p['head_sum'], p['r_k'], p['gn_g'], p['gn_b'], p['rwkv_w_o'], p['w_out'], p['ln1_g'], p['ln1_b'],
              p['router_hi'], p['router_cat'], p['router_b']]
    return pl.pallas_call(
        functools.partial(_mix_out_body, MIX_OUT_HALVES),
        out_shape=[jax.ShapeDtypeStruct((t, D_MODEL), F32), jax.ShapeDtypeStruct((t, D_MODEL), BF16),
                   jax.ShapeDtypeStruct((t, 128), F32), jax.ShapeDtypeStruct((8, t), F32),
                   jax.ShapeDtypeStruct((N_EXPERTS, 128), F32)],
        grid=(t // ts,),
        in_specs=[row(RW)] * 5 + [row(D_MODEL)] * 3 + [_const_spec(a.shape) for a in consts],
        out_specs=[row(D_MODEL), row(D_MODEL), row(128), pl.BlockSpec((8, ts), lambda i: (0, i)),
                   pl.BlockSpec((N_EXPERTS, 128), lambda i: (0, 0))],
        scratch_shapes=[pltpu.VMEM((N_EXPERTS, 128), F32)],
        compiler_params=pltpu.CompilerParams(dimension_semantics=("arbitrary",), vmem_limit_bytes=VMEM_LIMIT),
        name="mix_out",
    )(y, r, k, v, g, ga, part, x, *consts)


def _experts_body(be_ref, nb_ref, x_ref, wg_ref, wu_ref, wd_ref, o_ref, wg_s, wu_s, wd_s):
    i = pl.program_id(0)
    used = i < nb_ref[0]

    @pl.when(used & ((i == 0) | (be_ref[i] != be_ref[jnp.maximum(i - 1, 0)])))
    def _():
        wg_s[...] = _bf(wg_ref[...])
        wu_s[...] = _bf(wu_ref[...])
        wd_s[...] = _bf(wd_ref[...])

    @pl.when(used)
    def _():
        hb = EXPERT_BLOCK // 2
        rows = [pl.ds(h * hb, hb) for h in range(2)]
        xs = [x_ref[rs, :] for rs in rows]
        hg = [_dot(a, wg_s[...]) for a in xs]
        hu = [_dot(a, wu_s[...]) for a in xs]
        hh = [_bf(a * _sigmoid(a) * b) for a, b in zip(hg, hu)]
        out = [_dot(a, wd_s[...]) for a in hh]
        for rs, a in zip(rows, out):
            o_ref[rs, :] = _bf(a)

    @pl.when(jnp.logical_not(used))
    def _():
        o_ref[...] = jnp.zeros(o_ref.shape, BF16)


def _experts(xg, block_expert, n_used, wg, wu, wd, layer):
    rows = xg.shape[0]
    nb = rows // EXPERT_BLOCK
    grid_spec = pltpu.PrefetchScalarGridSpec(
        num_scalar_prefetch=2,
        grid=(nb,),
        in_specs=[
            pl.BlockSpec((EXPERT_BLOCK, D_MODEL), lambda i, be, nu: (i, 0)),
            pl.BlockSpec((None, None, D_MODEL, D_EXPERT), lambda i, be, nu: (layer, be[i], 0, 0)),
            pl.BlockSpec((None, None, D_MODEL, D_EXPERT), lambda i, be, nu: (layer, be[i], 0, 0)),
            pl.BlockSpec((None, None, D_EXPERT, D_MODEL), lambda i, be, nu: (layer, be[i], 0, 0)),
        ],
        out_specs=pl.BlockSpec((EXPERT_BLOCK, D_MODEL), lambda i, be, nu: (i, 0)),
        scratch_shapes=[pltpu.VMEM((D_MODEL, D_EXPERT), BF16), pltpu.VMEM((D_MODEL, D_EXPERT), BF16),
                        pltpu.VMEM((D_EXPERT, D_MODEL), BF16)],
    )
    return pl.pallas_call(
        _experts_body,
        out_shape=jax.ShapeDtypeStruct((rows, D_MODEL), BF16),
        grid_spec=grid_spec,
        compiler_params=pltpu.CompilerParams(dimension_semantics=("arbitrary",), vmem_limit_bytes=VMEM_LIMIT),
        name="experts",
    )(block_expert, n_used, xg, wg, wu, wd)


def _combine_body(x_ref, y0_ref, y1_ref, route_ref, g_ref, b_ref, o_ref):
    route = route_ref[...]
    moe = route[:, 2:3] * y0_ref[...] + route[:, 3:4] * y1_ref[...]
    o_ref[...] = _layer_norm(ALPHA * x_ref[...] + moe, g_ref[...], b_ref[...])


def _combine(x1, y01, route, ln_g, ln_b):
    t = x1.shape[0]
    ts = COMBINE_TILE
    row = lambda c: pl.BlockSpec((ts, c), lambda i: (i, 0))
    pick = lambda j: pl.BlockSpec((None, ts, D_MODEL), lambda i: (j, i, 0))
    return pl.pallas_call(
        _combine_body,
        out_shape=jax.ShapeDtypeStruct((t, D_MODEL), F32),
        grid=(t // ts,),
        in_specs=[row(D_MODEL), pick(0), pick(1), row(128), _const_spec(ln_g.shape), _const_spec(ln_b.shape)],
        out_specs=row(D_MODEL),
        compiler_params=pltpu.CompilerParams(dimension_semantics=("arbitrary",), vmem_limit_bytes=VMEM_LIMIT),
        name="combine",
    )(x1, y01, y01, route, ln_g, ln_b)


def _dispatch(route_t, counts, t):
    expert = route_t[0:2].astype(jnp.int32)
    rank = route_t[4:6].astype(jnp.int32)
    counts = counts[:, 0].astype(jnp.int32)
    padded = (counts + EXPERT_BLOCK - 1) // EXPERT_BLOCK * EXPERT_BLOCK
    pad_end = jnp.cumsum(padded)
    pad_start = pad_end - padded
    dest = rank
    for e in range(N_EXPERTS):
        dest = dest + jnp.where(expert == e, pad_start[e], 0)
    n_blocks = (2 * t + EXPERT_BLOCK - 1) // EXPERT_BLOCK + N_EXPERTS
    block_start = jnp.arange(n_blocks, dtype=jnp.int32) * EXPERT_BLOCK
    block_expert = jnp.minimum(jnp.sum((pad_end[None, :] <= block_start[:, None]).astype(jnp.int32), axis=1),
                               N_EXPERTS - 1)
    n_used = (pad_end[-1:] // EXPERT_BLOCK).astype(jnp.int32)
    by_row = jnp.argsort(dest.reshape(2 * t)).astype(jnp.int32) % t
    pad_before = jnp.cumsum(padded - counts) - (padded - counts)
    src = jnp.arange(n_blocks * EXPERT_BLOCK, dtype=jnp.int32) - jnp.repeat(pad_before[block_expert], EXPERT_BLOCK)
    buf_tok = jnp.take(by_row, src % (2 * t), mode='clip')
    return dest, buf_tok, block_expert, n_used


def _pad_cols(w, width=LORA_PAD):
    return jnp.pad(w, ((0, 0), (0, width - w.shape[1])))


def _pad_rows(w, height=LORA_PAD):
    return jnp.pad(w, ((0, height - w.shape[0]), (0, 0)))


def _row(v):
    return v.reshape(1, -1).astype(F32)


def _layer_params(l, a):
    w_in = a['w_in_first'] if l == 0 else a['w_in_rest'][l - 1]
    if l == 0:
        w_vres = jnp.zeros((D_MODEL, LORA_PAD), F32)
        mu_vres = jnp.zeros((LORA_PAD,), F32)
    else:
        w_vres = _pad_cols(w_in[:, _O_VRES:])
        mu_vres = jnp.pad(a['rwkv_mu_vres'][l - 1], (0, LORA_PAD - 32))
    mu = a['rwkv_mu'][l]
    p = {
        'w_rw': _bf(jnp.concatenate([w_in[:, :_O_WL], _pad_cols(w_in[:, _O_WL:_O_AL]), _pad_cols(w_in[:, _O_AL:_O_GL]),
                                     _pad_cols(w_in[:, _O_GL:_O_SGU]), w_vres], axis=1)),
        'w_sgu': _bf(w_in[:, _O_SGU:_O_CONV]),
        'w_conv': _bf(w_in[:, _O_CONV:_O_GATE]),
        'w_gate': _bf(w_in[:, _O_GATE:_O_VRES]),
        'mu': _row(jnp.concatenate([mu[:_O_WL], jnp.pad(mu[_O_WL:_O_AL], (0, 96)), jnp.pad(mu[_O_AL:_O_GL], (0, 96)),
                                    jnp.pad(mu[_O_GL:], (0, 32)), mu_vres])),
        'w0': _row(a['rwkv_w0'][l]),
        'w_up': _bf(_pad_rows(a['rwkv_w_up'][l])),
        'a0': _row(a['rwkv_a0'][l]),
        'a_up': _bf(_pad_rows(a['rwkv_a_up'][l])),
        'g_up': _bf(_pad_rows(a['rwkv_g_up'][l])),
        'k_k': _row(a['rwkv_k_k'][l]),
        'k_a': _row(a['rwkv_k_a'][l]),
        'r_k': _row(a['rwkv_r_k'][l]),
        'gn_g': _row(a['rwkv_gn_g'][l]),
        'gn_b': _row(a['rwkv_gn_b'][l]),
        'rwkv_w_o': _bf(a['rwkv_w_o'][l]),
        'sgu_ln_g': _row(a['sgu_ln_g'][l]),
        'sgu_ln_b': _row(a['sgu_ln_b'][l]),
        'sgu_w': jnp.transpose(a['sgu_w'][l], (1, 0, 2)).reshape(SGU_CHUNK, SGU_GROUPS * SGU_CHUNK),
        'sgu_bias': jnp.repeat(jnp.transpose(a['sgu_b'][l]), SGU_W // SGU_GROUPS, axis=1),
        'sgu_w_o': _bf(a['sgu_w_o'][l]),
        'conv_dw': jnp.pad(a['conv_dw'][l], ((0, 1), (0, 0))),
        'conv_db': _row(a['conv_db'][l]),
        'conv_ln_g': _row(a['conv_ln_g'][l]),
        'conv_ln_b': _row(a['conv_ln_b'][l]),
        'conv_w_o': _bf(a['conv_w_o'][l]),
        'w_out': _bf(a['w_out'][l]),
        'ln1_g': _row(a['ln1_g'][l]),
        'ln1_b': _row(a['ln1_b'][l]),
        'ln2_g': _row(a['ln2_g'][l]),
        'ln2_b': _row(a['ln2_b'][l]),
    }
    if l > 0:
        p['v0'] = _row(a['rwkv_v0'][l - 1])
        p['v_up'] = _bf(_pad_rows(a['rwkv_v_up'][l - 1]))
    return p


def _forward(a):
    x = a['x']
    bsz, seq, _ = x.shape
    t = bsz * seq
    x = x.reshape(t, D_MODEL)
    head = jnp.arange(HEAD_SLAB, dtype=jnp.int32) // HEAD_DIM
    head_sum = _bf((head[:, None] == head[None, :]).astype(F32))
    rw = _pad_cols(a['router_w'])
    rw_hi = _bf(rw)
    rw_lo = _bf(rw - rw_hi.astype(F32))
    rw_cat = jnp.concatenate([rw_hi, rw_lo], axis=1)
    rb = jnp.broadcast_to(a['router_b'].astype(F32)[:, None], (N_EXPERTS, 128))
    sb = bsz // STREAMS
    st = sb * seq
    xs = [x[i * st:(i + 1) * st] for i in range(STREAMS)]
    v_first = [None] * STREAMS
    for l in range(DEPTH):
        p = _layer_params(l, a)
        p.update(head_sum=head_sum, router_hi=rw_hi, router_cat=rw_cat, router_b=rb)
        for i in range(STREAMS):
            r, lw, k, v, na, nb, g, ga, part = _mix_in(xs[i], p, v_first[i], sb, seq)
            if l == 0:
                v_first[i] = v
            y = _wkv(r, lw, k, v, na, nb, sb, seq)
            x1, x1b, route, route_t, counts = _mix_out(y, r, k, v, g, ga, part, xs[i], p, sb, seq)
            dest, buf_tok, block_expert, n_used = _dispatch(route_t, counts, st)
            xg = jnp.take(x1b, buf_tok, axis=0, mode='clip')
            yb = _experts(xg, block_expert, n_used, a['moe_w_gate'], a['moe_w_up'], a['moe_w_down'], l)
            y01 = jnp.take(yb, dest.reshape(2 * st), axis=0, mode='clip').reshape(2, st, D_MODEL)
            xs[i] = _combine(x1, y01, route, p['ln2_g'], p['ln2_b'])
    return jnp.concatenate(xs, axis=0).reshape(bsz, seq, D_MODEL)


def kernel(x, w_in_first, w_in_rest, rwkv_mu, rwkv_mu_vres, rwkv_w0, rwkv_w_up, rwkv_a0, rwkv_a_up, rwkv_g_up, rwkv_v0, rwkv_v_up, rwkv_k_k, rwkv_k_a, rwkv_r_k, rwkv_gn_g, rwkv_gn_b, rwkv_w_o, sgu_ln_g, sgu_ln_b, sgu_w, sgu_b, sgu_w_o, conv_dw, conv_db, conv_ln_g, conv_ln_b, conv_w_o, w_out, ln1_g, ln1_b, router_w, router_b, moe_w_gate, moe_w_up, moe_w_down, ln2_g, ln2_b):
    return _forward(dict(
        x=x, w_in_first=w_in_first, w_in_rest=w_in_rest, rwkv_mu=rwkv_mu, rwkv_mu_vres=rwkv_mu_vres,
        rwkv_w0=rwkv_w0, rwkv_w_up=rwkv_w_up, rwkv_a0=rwkv_a0, rwkv_a_up=rwkv_a_up, rwkv_g_up=rwkv_g_up,
        rwkv_v0=rwkv_v0, rwkv_v_up=rwkv_v_up, rwkv_k_k=rwkv_k_k, rwkv_k_a=rwkv_k_a, rwkv_r_k=rwkv_r_k,
        rwkv_gn_g=rwkv_gn_g, rwkv_gn_b=rwkv_gn_b, rwkv_w_o=rwkv_w_o, sgu_ln_g=sgu_ln_g, sgu_ln_b=sgu_ln_b,
        sgu_w=sgu_w, sgu_b=sgu_b, sgu_w_o=sgu_w_o, conv_dw=conv_dw, conv_db=conv_db, conv_ln_g=conv_ln_g,
        conv_ln_b=conv_ln_b, conv_w_o=conv_w_o, w_out=w_out, ln1_g=ln1_g, ln1_b=ln1_b, router_w=router_w,
        router_b=router_b, moe_w_gate=moe_w_gate, moe_w_up=moe_w_up, moe_w_down=moe_w_down, ln2_g=ln2_g,
        ln2_b=ln2_b))
```

```python
import functools

import jax
import jax.numpy as jnp
from jax import lax
from jax.experimental import pallas as pl
from jax.experimental.pallas import tpu as pltpu

F32 = jnp.float32
BF16 = jnp.bfloat16

D_MODEL = 1024
DEPTH = 4
HEADS = 8
HEAD_DIM = 64
RW = HEADS * HEAD_DIM
LORA_PAD = 128
RW_COLS = 3 * RW + 4 * LORA_PAD
SGU_W = 256
SGU_CHUNK = 128
SGU_GROUPS = 4
CONV_W = 256
CONV_K = 31
N_EXPERTS = 32
GROUP_SIZE = 8
N_GROUPS = 4
D_EXPERT = 512
EXPERT_BLOCK = 512
ALPHA = (2 * DEPTH) ** 0.25
LN_EPS = 1e-5
GN_EPS = 64e-5

_O_WL = 3 * RW
_O_AL = _O_WL + 32
_O_GL = _O_AL + 32
_O_SGU = _O_GL + 96
_O_CONV = _O_SGU + 2 * SGU_W
_O_GATE = _O_CONV + 2 * CONV_W
_O_VRES = _O_GATE + 3 * D_MODEL

SEQ_TILE = 256
WKV_CHUNK = 64
WKV_TILE = 256
MIX_OUT_HALVES = 2
COMBINE_TILE = 512
STREAMS = 2
PAIR = 2 * HEAD_DIM
HEAD_SLAB = 256
VMEM_LIMIT = 48 * 1024 * 1024


def _dot(a, b):
    return jnp.dot(a, b, preferred_element_type=F32)


def _dot_nt(a, b):
    return lax.dot_general(a, b, (((1,), (1,)), ((), ())), preferred_element_type=F32)


def _bf(x):
    return x.astype(BF16)


def _head_sum(x_bf16, ones_bd):
    w = ones_bd.shape[0]
    return jnp.concatenate([_dot(x_bf16[:, i:i + w], ones_bd) for i in range(0, x_bf16.shape[1], w)], axis=1)


def _head_sum_split(x, ones_bd):
    hi = _bf(x)
    lo = _bf(x - hi.astype(F32))
    return _head_sum(hi, ones_bd) + _head_sum(lo, ones_bd)


def _split_dot_left(w_bf16, x):
    hi = _bf(x)
    lo = _bf(x - hi.astype(F32))
    return _dot(w_bf16, hi) + _dot(w_bf16, lo)


def _layer_norm(x, g, b):
    mu = jnp.mean(x, axis=-1, keepdims=True)
    d = x - mu
    var = jnp.mean(d * d, axis=-1, keepdims=True)
    return d * lax.rsqrt(var + LN_EPS) * g + b


def _sigmoid(x):
    return 1.0 / (1.0 + jnp.exp(-x))


def _const_spec(shape):
    nd = len(shape)
    return pl.BlockSpec(shape, lambda *_: (0,) * nd, pipeline_mode=pl.Buffered(1))


def _mix_in_body(has_vres, ts, *refs):
    it = iter(refs)
    x_ref, wrw_ref, wsgu_ref, wconv_ref, wgate_ref = (next(it) for _ in range(5))
    mu_ref, w0_ref, wup_ref, a0_ref, aup_ref, gup_ref, kk_ref, ka_ref = (next(it) for _ in range(8))
    if has_vres:
        v0_ref, vup_ref, vfirst_ref = (next(it) for _ in range(3))
    hs_ref = next(it)
    slng_ref, slnb_ref, sw_ref, sbias_ref, swo_ref = (next(it) for _ in range(5))
    cdw_ref, cdb_ref, clng_ref, clnb_ref, cwo_ref = (next(it) for _ in range(5))
    r_o, lw_o, k_o, v_o, a_o, b_o, g_o, ga_o, part_o = (next(it) for _ in range(9))
    pbuf, hext, hshift = next(it), next(it), next(it)

    @pl.when(pl.program_id(1) == 0)
    def _():
        pbuf[0:8, :] = jnp.zeros((8, RW_COLS), F32)
        hext[0:32, :] = jnp.zeros((32, CONV_W), F32)

    xb = _bf(x_ref[...])
    pr = _dot(xb, wrw_ref[...])
    pb = _dot(xb, wsgu_ref[...])
    pc = _dot(xb, wconv_ref[...])
    gate_logits = _dot(xb, wgate_ref[...])

    pbuf[8:8 + ts, :] = pr
    prev = pbuf[pl.ds(7, ts), :]
    pa = pr + (prev - pr) * mu_ref[...]
    pbuf[0:8, :] = pbuf[ts:ts + 8, :]
    r = pa[:, 0:RW]
    k = pa[:, RW:2 * RW]
    v = pa[:, 2 * RW:3 * RW]
    wl = pa[:, 3 * RW:3 * RW + LORA_PAD]
    al = pa[:, 3 * RW + LORA_PAD:3 * RW + 2 * LORA_PAD]
    gl = pa[:, 3 * RW + 2 * LORA_PAD:3 * RW + 3 * LORA_PAD]
    z = w0_ref[...] + _dot(_bf(jnp.tanh(wl)), wup_ref[...])
    nz = -z
    softplus = jnp.maximum(nz, 0.0) + jnp.log(1.0 + jnp.exp(-jnp.abs(nz)))
    lw_o[...] = -jnp.exp(-softplus - 0.5)
    iclr = _sigmoid(a0_ref[...] + _dot(_bf(al), aup_ref[...]))
    g_o[...] = _bf(_dot(_bf(_sigmoid(gl)), gup_ref[...]))
    kkv = k * kk_ref[...]
    ss = _head_sum(_bf(kkv * kkv), hs_ref[...])
    kkn = kkv / jnp.maximum(jnp.sqrt(ss), 1e-12)
    k_o[...] = _bf(k * (1.0 + (iclr - 1.0) * ka_ref[...]))
    if has_vres:
        vl = pa[:, 3 * RW + 3 * LORA_PAD:RW_COLS]
        v = v + (vfirst_ref[...] - v) * _sigmoid(v0_ref[...] + _dot(_bf(vl), vup_ref[...]))
    r_o[...] = _bf(r)
    v_o[...] = _bf(v)
    a_o[...] = _bf(-kkn)
    b_o[...] = _bf(kkn * iclr)

    zz = 0.5 * pb * (1.0 + lax.erf(pb * (2.0 ** -0.5)))
    u = zz[:, :SGU_W]
    vv = _layer_norm(zz[:, SGU_W:], slng_ref[...], slnb_ref[...])
    prow = lax.broadcasted_iota(jnp.int32, (SGU_CHUNK, SGU_GROUPS * SGU_CHUNK), 0)
    qcol = lax.broadcasted_iota(jnp.int32, (SGU_CHUNK, SGU_GROUPS * SGU_CHUNK), 1) % SGU_CHUNK
    wc = _bf(jnp.where(qcol <= prow, sw_ref[...], 0.0))
    lane_grp = lax.broadcasted_iota(jnp.int32, (SGU_CHUNK, SGU_W), 1) // (SGU_W // SGU_GROUPS)
    mixed = []
    for c in range(ts // SGU_CHUNK):
        vc = vv[c * SGU_CHUNK:(c + 1) * SGU_CHUNK]
        stack = jnp.concatenate([jnp.where(lane_grp == g, vc, 0.0) for g in range(SGU_GROUPS)], axis=0)
        mixed.append(_dot(wc, _bf(stack)) + sbias_ref[...])
    mixed = jnp.concatenate(mixed, axis=0)
    o_b = _dot(_bf(u * mixed), swo_ref[...])

    hext[32:32 + ts, :] = pc[:, :CONV_W] * _sigmoid(pc[:, CONV_W:])
    acc = jnp.zeros((ts, CONV_W), F32) + cdb_ref[...]
    first = 32 - (CONV_K - 1)
    for rem in range(8):
        taps = [j for j in range(CONV_K) if (first + j) % 8 == rem]
        span = max(first + j for j in taps) - rem
        hshift[0:span + ts, :] = hext[pl.ds(rem, span + ts), :]
        for j in taps:
            off = first + j - rem
            acc = acc + cdw_ref[j:j + 1, :] * hshift[off:off + ts, :]
    hext[0:32, :] = hext[ts:ts + 32, :]
    hc = _layer_norm(acc, clng_ref[...], clnb_ref[...])
    hc = hc * _sigmoid(hc)
    o_c = _dot(_bf(hc), cwo_ref[...])

    gates = _sigmoid(_bf(gate_logits))
    ga_o[...] = gates[:, :D_MODEL]
    part_o[...] = _bf(gates[:, D_MODEL:2 * D_MODEL] * o_b + gates[:, 2 * D_MODEL:] * o_c)


def _mix_in(x, p, v_first, bsz, seq):
    t = bsz * seq
    ts = SEQ_TILE
    nst = seq // ts
    has_vres = v_first is not None
    row = lambda c: pl.BlockSpec((ts, c), lambda b, s: (b * nst + s, 0))
    args = [x, p['w_rw'], p['w_sgu'], p['w_conv'], p['w_gate'], p['mu'], p['w0'], p['w_up'], p['a0'], p['a_up'],
            p['g_up'], p['k_k'], p['k_a']]
    specs = [row(D_MODEL)] + [_const_spec(a.shape) for a in args[1:]]
    if has_vres:
        args += [p['v0'], p['v_up'], v_first]
        specs += [_const_spec(p['v0'].shape), _const_spec(p['v_up'].shape), row(RW)]
    tail = [p['head_sum'], p['sgu_ln_g'], p['sgu_ln_b'], p['sgu_w'], p['sgu_bias'], p['sgu_w_o'],
            p['conv_dw'], p['conv_db'], p['conv_ln_g'], p['conv_ln_b'], p['conv_w_o']]
    args += tail
    specs += [_const_spec(a.shape) for a in tail]
    out_shape = ([jax.ShapeDtypeStruct((t, RW), F32 if i == 1 else BF16) for i in range(7)]
                 + [jax.ShapeDtypeStruct((t, D_MODEL), BF16)] * 2)
    out_specs = [row(RW)] * 7 + [row(D_MODEL)] * 2
    return pl.pallas_call(
        functools.partial(_mix_in_body, has_vres, ts),
        out_shape=out_shape,
        grid=(bsz, nst),
        in_specs=specs,
        out_specs=out_specs,
        scratch_shapes=[pltpu.VMEM((ts + 8, RW_COLS), F32), pltpu.VMEM((ts + 32, CONV_W), F32),
                        pltpu.VMEM((ts + 32, CONV_W), F32)],
        compiler_params=pltpu.CompilerParams(dimension_semantics=("arbitrary", "arbitrary"),
                                             vmem_limit_bytes=VMEM_LIMIT),
        name="mix_in",
    )(*args)


def _wkv_body(tb, r_ref, lw_ref, k_ref, v_ref, a_ref, b_ref, y_ref, st_ref, wr_s, rs_s, ar_s, vs_s, kb_s, dc_s):
    c = WKV_CHUNK
    nc = tb // c
    npair = HEADS // 2
    units = [(ci, p) for ci in range(nc) for p in range(npair)]

    @pl.when(pl.program_id(1) == 0)
    def _():
        st_ref[...] = jnp.zeros(st_ref.shape, F32)

    lane = lax.broadcasted_iota(jnp.int32, (c, PAIR), 1)
    head0 = lane < HEAD_DIM
    ti = lax.broadcasted_iota(jnp.int32, (PAIR, PAIR), 0)
    si = lax.broadcasted_iota(jnp.int32, (PAIR, PAIR), 1)
    same = (ti // c) == (si // c)
    lower = same & ((si % c) <= (ti % c))
    strict = same & ((si % c) < (ti % c))
    eye = (ti == si).astype(F32)
    tr = lax.broadcasted_iota(jnp.int32, (tb, tb), 0)
    tc = lax.broadcasted_iota(jnp.int32, (tb, tb), 1)
    tri = _bf(((tr // c == tc // c) & (tc <= tr)).astype(F32))

    def stack(x):
        return jnp.concatenate([jnp.where(head0, x, 0.0), jnp.where(head0, 0.0, x)], axis=0)

    lw = lw_ref[...]
    cum = _split_dot_left(tri, lw)
    cum_end = jnp.concatenate(
        [jnp.broadcast_to(cum[(ci + 1) * c - 1:(ci + 1) * c, :], (c, RW)) for ci in range(nc)], axis=0)
    p_inv = jnp.exp(-cum)
    p_end = jnp.exp(cum_end - cum)
    rt = r_ref[...] * jnp.exp(cum)
    at = a_ref[...] * jnp.exp(cum - lw)
    kt = k_ref[...] * p_inv
    bt = b_ref[...] * p_inv
    ke = k_ref[...] * p_end
    be = b_ref[...] * p_end
    vt = v_ref[...]

    def sub(x, ci, p):
        return x[ci * c:(ci + 1) * c, p * PAIR:(p + 1) * PAIR]

    a_s = [stack(sub(at, ci, p)) for ci, p in units]
    r_s = [stack(sub(rt, ci, p)) for ci, p in units]
    g = [_dot_nt(_bf(jnp.concatenate([r_s[i], a_s[i]], axis=0)),
                 _bf(jnp.concatenate([stack(sub(kt, ci, p)), stack(sub(bt, ci, p))], axis=0)))
         for i, (ci, p) in enumerate(units)]
    for i, (ci, p) in enumerate(units):
        vs_s[ci, p] = _bf(stack(sub(vt, ci, p)))
        ar_s[ci, p] = _bf(jnp.concatenate([jnp.where(lower, g[i][0:PAIR, 0:PAIR], 0.0),
                                           jnp.where(lower, g[i][0:PAIR, PAIR:], 0.0)], axis=1))
        kb_s[ci, p] = _bf(jnp.concatenate([stack(sub(ke, ci, p)).T, stack(sub(be, ci, p)).T], axis=1))
        dc_s[ci, p] = jnp.exp(jnp.broadcast_to(sub(cum_end, ci, p)[0:1, :], (PAIR, PAIR))).T
    apow = [jnp.where(strict, g[i][PAIR:, PAIR:], 0.0) for i in range(len(units))]
    inv = [eye + x for x in apow]
    apb = [_bf(x) for x in apow]
    apow = [_dot(x, x) for x in apb]
    for _ in range(4):
        apb = [_bf(x) for x in apow]
        both = [_dot(x, jnp.concatenate([x, _bf(y)], axis=1)) for x, y in zip(apb, inv)]
        apow = [x[:, :PAIR] for x in both]
        inv = [y + x[:, PAIR:] for x, y in zip(both, inv)]
    inv = [y + _dot(_bf(x), _bf(y)) for x, y in zip(apow, inv)]
    for i, (ci, p) in enumerate(units):
        a_ak = jnp.where(strict, g[i][PAIR:, 0:PAIR], 0.0)
        wr_s[ci, p] = _bf(_dot(_bf(inv[i]), _bf(jnp.concatenate([a_s[i], a_ak], axis=1))))
        rs_s[ci, p] = _bf(r_s[i])

    s = [st_ref[p] for p in range(npair)]
    for ci in range(nc):
        sb = [_bf(x) for x in s]
        u = [_dot(wr_s[ci, p], jnp.concatenate([sb[p], vs_s[ci, p]], axis=0)) for p in range(npair)]
        vu = [jnp.concatenate([vs_s[ci, p], _bf(u[p])], axis=0) for p in range(npair)]
        s = [dc_s[ci, p] * s[p] + _dot(kb_s[ci, p], vu[p]) for p in range(npair)]
        for p in range(npair):
            y = _dot(rs_s[ci, p], sb[p]) + _dot(ar_s[ci, p], vu[p])
            y_ref[ci * c:(ci + 1) * c, p * PAIR:(p + 1) * PAIR] = y[0:c] + y[c:]
    for p in range(npair):
        st_ref[p] = s[p]


def _wkv(r, lw, k, v, a, b, bsz, seq):
    t = bsz * seq
    tb = WKV_TILE
    nst = seq // tb
    nc, npair = tb // WKV_CHUNK, HEADS // 2
    row = pl.BlockSpec((tb, RW), lambda bi, s: (bi * nst + s, 0))
    return pl.pallas_call(
        functools.partial(_wkv_body, tb),
        out_shape=jax.ShapeDtypeStruct((t, RW), F32),
        grid=(bsz, nst),
        in_specs=[row] * 6,
        out_specs=row,
        scratch_shapes=[pltpu.VMEM((npair, PAIR, PAIR), F32),
                        pltpu.VMEM((nc, npair, PAIR, 2 * PAIR), BF16),
                        pltpu.VMEM((nc, npair, PAIR, PAIR), BF16),
                        pltpu.VMEM((nc, npair, PAIR, 2 * PAIR), BF16),
                        pltpu.VMEM((nc, npair, PAIR, PAIR), BF16),
                        pltpu.VMEM((nc, npair, PAIR, 2 * PAIR), BF16),
                        pltpu.VMEM((nc, npair, PAIR, PAIR), F32)],
        compiler_params=pltpu.CompilerParams(dimension_semantics=("arbitrary", "arbitrary"),
                                             vmem_limit_bytes=VMEM_LIMIT),
        name="wkv",
    )(r, lw, k, v, a, b)


def _mix_out_body(halves, y_ref, r_ref, k_ref, v_ref, g_ref, ga_ref, part_ref, x_ref, hs_ref, rk_ref, gng_ref,
                  gnb_ref, wo_ref, wout_ref, ln_g_ref, ln_b_ref, rw_hi_ref, rw_cat_ref, rb_ref, x1_o, x1b_o, route_o,
                  route_t_o, counts_o, cnt_ref):
    th = y_ref.shape[0] // halves
    hrows = [pl.ds(h * th, th) for h in range(halves)]

    def each(fn, *cols):
        return [fn(*a) for a in zip(*cols)]

    def load(ref):
        return [ref[rs, :] for rs in hrows]

    hs = hs_ref[...]
    inv_n = 1.0 / HEAD_DIM
    y = load(y_ref)
    mu = each(lambda a: _head_sum_split(a, hs) * inv_n, y)
    d = each(lambda a, m: a - m, y, mu)
    var = each(lambda a: _head_sum(_bf(a * a), hs) * inv_n, d)
    rkk = each(lambda a, b: _head_sum(_bf(a * b * rk_ref[...]), hs), load(r_ref), load(k_ref))
    yn = each(lambda a, s2: a * lax.rsqrt(s2 + GN_EPS) * gng_ref[...] + gnb_ref[...], d, var)
    out = each(lambda a, bo, vv, gg: _bf((a + bo * vv) * gg), yn, rkk, load(v_ref), load(g_ref))
    o_a = each(lambda a: _dot(a, wo_ref[...]), out)
    merged = each(lambda a, b, c2: _bf(a * b + c2), load(ga_ref), o_a, load(part_ref))
    xm = each(lambda a: _dot(a, wout_ref[...]), merged)
    x1 = each(lambda a, b: _layer_norm(ALPHA * a + b, ln_g_ref[...], ln_b_ref[...]), load(x_ref), xm)
    xh = each(_bf, x1)
    for rs, a, b in zip(hrows, x1, xh):
        x1_o[rs, :] = a
        x1b_o[rs, :] = b

    xl = each(lambda a, b: _bf(a - b.astype(F32)), x1, xh)
    hcat = each(lambda a: _dot(a, rw_cat_ref[...]), xh)
    logits = each(lambda c2, b: c2[:, :128] + c2[:, 128:] + _dot(b, rw_hi_ref[...]), hcat, xl)
    lt = each(lambda a: a.T[0:N_EXPERTS, :], logits)
    scores = each(_sigmoid, lt)
    bias = jnp.concatenate([rb_ref[...]] * (th // 128), axis=1)
    sel = each(lambda a: a + bias, scores)
    row = lax.broadcasted_iota(jnp.int32, (N_EXPERTS, th), 0).astype(F32)
    grp_row = lax.broadcasted_iota(jnp.int32, (GROUP_SIZE, th), 0).astype(F32)
    neg = jnp.float32(-jnp.inf)

    def top2(sg, rg):
        m1 = jnp.max(sg, axis=0, keepdims=True)
        i1 = jnp.min(jnp.where(sg == m1, rg, 1e9), axis=0, keepdims=True)
        sg2 = jnp.where(rg == i1, neg, sg)
        m2 = jnp.max(sg2, axis=0, keepdims=True)
        i2 = jnp.min(jnp.where(sg2 == m2, rg, 1e9), axis=0, keepdims=True)
        return m1 + m2, i1, i2

    best = e0 = e1 = None
    for g in range(N_GROUPS):
        grp = slice(g * GROUP_SIZE, (g + 1) * GROUP_SIZE)
        cand = each(lambda a: top2(a[grp], grp_row + float(g * GROUP_SIZE)), sel)
        if g == 0:
            best, e0, e1 = ([c[i] for c in cand] for i in range(3))
        else:
            take = each(lambda c, b: c[0] > b, cand, best)
            best = each(lambda tk, c, b: jnp.where(tk, c[0], b), take, cand, best)
            e0 = each(lambda tk, c, b: jnp.where(tk, c[1], b), take, cand, e0)
            e1 = each(lambda tk, c, b: jnp.where(tk, c[2], b), take, cand, e1)
    g0 = each(lambda e, sc: jnp.sum(jnp.where(row == e, sc, 0.0), axis=0, keepdims=True), e0, scores)
    g1 = each(lambda e, sc: jnp.sum(jnp.where(row == e, sc, 0.0), axis=0, keepdims=True), e1, scores)

    @pl.when(pl.program_id(0) == 0)
    def _():
        cnt_ref[...] = jnp.zeros(cnt_ref.shape, F32)

    chosen = each(lambda a, b: jnp.where((row == a) | (row == b), 1.0, 0.0), e0, e1)
    earlier = _bf((lax.broadcasted_iota(jnp.int32, (th, th), 0)
                   < lax.broadcasted_iota(jnp.int32, (th, th), 1)).astype(F32))
    inside = each(lambda a: _dot(_bf(a), earlier), chosen)
    total = each(lambda a: jnp.sum(a, axis=1, keepdims=True), chosen)
    base = cnt_ref[...]
    row8 = lax.broadcasted_iota(jnp.int32, (8, th), 0)
    for h in range(halves):
        seen = inside[h] + jnp.concatenate([base] * (th // 128), axis=1)
        rank0 = jnp.sum(jnp.where(row == e0[h], seen, 0.0), axis=0, keepdims=True)
        rank1 = jnp.sum(jnp.where(row == e1[h], seen, 0.0), axis=0, keepdims=True)
        tot = g0[h] + g1[h]
        route_t = jnp.where(row8 == 0, e0[h], jnp.where(row8 == 1, e1[h], jnp.where(
            row8 == 2, g0[h] / tot, jnp.where(row8 == 3, g1[h] / tot, jnp.where(
                row8 == 4, rank0, jnp.where(row8 == 5, rank1, 0.0))))))
        route_t_o[:, hrows[h]] = route_t
        route_o[hrows[h], :] = jnp.concatenate([route_t, jnp.zeros((120, th), F32)], axis=0).T
        base = base + total[h]
    cnt_ref[...] = base
    counts_o[...] = base


def _mix_out(y, r, k, v, g, ga, part, x, p, bsz, seq):
    t = bsz * seq
    ts = MIX_OUT_HALVES * SEQ_TILE
    row = lambda c: pl.BlockSpec((ts, c), lambda i: (i, 0))
    consts = [p['head_sum'], p['r_k'], p['gn_g'], p['gn_b'], p['rwkv_w_o'], p['w_out'], p['ln1_g'], p['ln1_b'],
              p['router_hi'], p['router_cat'], p['router_b']]
    return pl.pallas_call(
        functools.partial(_mix_out_body, MIX_OUT_HALVES),
        out_shape=[jax.ShapeDtypeStruct((t, D_MODEL), F32), jax.ShapeDtypeStruct((t, D_MODEL), BF16),
                   jax.ShapeDtypeStruct((t, 128), F32), jax.ShapeDtypeStruct((8, t), F32),
                   jax.ShapeDtypeStruct((N_EXPERTS, 128), F32)],
        grid=(t // ts,),
        in_specs=[row(RW)] * 5 + [row(D_MODEL)] * 3 + [_const_spec(a.shape) for a in consts],
        out_specs=[row(D_MODEL), row(D_MODEL), row(128), pl.BlockSpec((8, ts), lambda i: (0, i)),
                   pl.BlockSpec((N_EXPERTS, 128), lambda i: (0, 0))],
        scratch_shapes=[pltpu.VMEM((N_EXPERTS, 128), F32)],
        compiler_params=pltpu.CompilerParams(dimension_semantics=("arbitrary",), vmem_limit_bytes=VMEM_LIMIT),
        name="mix_out",
    )(y, r, k, v, g, ga, part, x, *consts)


def _experts_body(be_ref, nb_ref, x_ref, wg_ref, wu_ref, wd_ref, o_ref, wg_s, wu_s, wd_s):
    i = pl.program_id(0)
    used = i < nb_ref[0]

    @pl.when(used & ((i == 0) | (be_ref[i] != be_ref[jnp.maximum(i - 1, 0)])))
    def _():
        wg_s[...] = _bf(wg_ref[...])
        wu_s[...] = _bf(wu_ref[...])
        wd_s[...] = _bf(wd_ref[...])

    @pl.when(used)
    def _():
        hb = EXPERT_BLOCK // 2
        rows = [pl.ds(h * hb, hb) for h in range(2)]
        xs = [x_ref[rs, :] for rs in rows]
        hg = [_dot(a, wg_s[...]) for a in xs]
        hu = [_dot(a, wu_s[...]) for a in xs]
        hh = [_bf(a * _sigmoid(a) * b) for a, b in zip(hg, hu)]
        out = [_dot(a, wd_s[...]) for a in hh]
        for rs, a in zip(rows, out):
            o_ref[rs, :] = _bf(a)

    @pl.when(jnp.logical_not(used))
    def _():
        o_ref[...] = jnp.zeros(o_ref.shape, BF16)


def _experts(xg, block_expert, n_used, wg, wu, wd, layer):
    rows = xg.shape[0]
    nb = rows // EXPERT_BLOCK
    grid_spec = pltpu.PrefetchScalarGridSpec(
        num_scalar_prefetch=2,
        grid=(nb,),
        in_specs=[
            pl.BlockSpec((EXPERT_BLOCK, D_MODEL), lambda i, be, nu: (i, 0)),
            pl.BlockSpec((None, None, D_MODEL, D_EXPERT), lambda i, be, nu: (layer, be[i], 0, 0)),
            pl.BlockSpec((None, None, D_MODEL, D_EXPERT), lambda i, be, nu: (layer, be[i], 0, 0)),
            pl.BlockSpec((None, None, D_EXPERT, D_MODEL), lambda i, be, nu: (layer, be[i], 0, 0)),
        ],
        out_specs=pl.BlockSpec((EXPERT_BLOCK, D_MODEL), lambda i, be, nu: (i, 0)),
        scratch_shapes=[pltpu.VMEM((D_MODEL, D_EXPERT), BF16), pltpu.VMEM((D_MODEL, D_EXPERT), BF16),
                        pltpu.VMEM((D_EXPERT, D_MODEL), BF16)],
    )
    return pl.pallas_call(
        _experts_body,
        out_shape=jax.ShapeDtypeStruct((rows, D_MODEL), BF16),
        grid_spec=grid_spec,
        compiler_params=pltpu.CompilerParams(dimension_semantics=("arbitrary",), vmem_limit_bytes=VMEM_LIMIT),
        name="experts",
    )(block_expert, n_used, xg, wg, wu, wd)


def _combine_body(x_ref, y0_ref, y1_ref, route_ref, g_ref, b_ref, o_ref):
    route = route_ref[...]
    moe = route[:, 2:3] * y0_ref[...] + route[:, 3:4] * y1_ref[...]
    o_ref[...] = _layer_norm(ALPHA * x_ref[...] + moe, g_ref[...], b_ref[...])


def _combine(x1, y0, y1, route, ln_g, ln_b):
    t = x1.shape[0]
    ts = COMBINE_TILE
    row = lambda c: pl.BlockSpec((ts, c), lambda i: (i, 0))
    return pl.pallas_call(
        _combine_body,
        out_shape=jax.ShapeDtypeStruct((t, D_MODEL), F32),
        grid=(t // ts,),
        in_specs=[row(D_MODEL)] * 3 + [row(128), _const_spec(ln_g.shape), _const_spec(ln_b.shape)],
        out_specs=row(D_MODEL),
        compiler_params=pltpu.CompilerParams(dimension_semantics=("arbitrary",), vmem_limit_bytes=VMEM_LIMIT),
        name="combine",
    )(x1, y0, y1, route, ln_g, ln_b)


def _dispatch(route_t, counts, t):
    expert = route_t[0:2].astype(jnp.int32)
    rank = route_t[4:6].astype(jnp.int32)
    counts = counts[:, 0].astype(jnp.int32)
    padded = (counts + EXPERT_BLOCK - 1) // EXPERT_BLOCK * EXPERT_BLOCK
    pad_end = jnp.cumsum(padded)
    pad_start = pad_end - padded
    dest = rank
    for e in range(N_EXPERTS):
        dest = dest + jnp.where(expert == e, pad_start[e], 0)
    n_blocks = (2 * t + EXPERT_BLOCK - 1) // EXPERT_BLOCK + N_EXPERTS
    block_start = jnp.arange(n_blocks, dtype=jnp.int32) * EXPERT_BLOCK
    block_expert = jnp.minimum(jnp.sum((pad_end[None, :] <= block_start[:, None]).astype(jnp.int32), axis=1),
                               N_EXPERTS - 1)
    n_used = (pad_end[-1:] // EXPERT_BLOCK).astype(jnp.int32)
    by_row = jnp.argsort(dest.reshape(2 * t)).astype(jnp.int32) % t
    pad_before = jnp.cumsum(padded - counts) - (padded - counts)
    src = jnp.arange(n_blocks * EXPERT_BLOCK, dtype=jnp.int32) - jnp.repeat(pad_before[block_expert], EXPERT_BLOCK)
    buf_tok = jnp.take(by_row, src % (2 * t), mode='clip')
    return dest, buf_tok, block_expert, n_used


def _pad_cols(w, width=LORA_PAD):
    return jnp.pad(w, ((0, 0), (0, width - w.shape[1])))


def _pad_rows(w, height=LORA_PAD):
    return jnp.pad(w, ((0, height - w.shape[0]), (0, 0)))


def _row(v):
    return v.reshape(1, -1).astype(F32)


def _layer_params(l, a):
    w_in = a['w_in_first'] if l == 0 else a['w_in_rest'][l - 1]
    if l == 0:
        w_vres = jnp.zeros((D_MODEL, LORA_PAD), F32)
        mu_vres = jnp.zeros((LORA_PAD,), F32)
    else:
        w_vres = _pad_cols(w_in[:, _O_VRES:])
        mu_vres = jnp.pad(a['rwkv_mu_vres'][l - 1], (0, LORA_PAD - 32))
    mu = a['rwkv_mu'][l]
    p = {
        'w_rw': _bf(jnp.concatenate([w_in[:, :_O_WL], _pad_cols(w_in[:, _O_WL:_O_AL]), _pad_cols(w_in[:, _O_AL:_O_GL]),
                                     _pad_cols(w_in[:, _O_GL:_O_SGU]), w_vres], axis=1)),
        'w_sgu': _bf(w_in[:, _O_SGU:_O_CONV]),
        'w_conv': _bf(w_in[:, _O_CONV:_O_GATE]),
        'w_gate': _bf(w_in[:, _O_GATE:_O_VRES]),
        'mu': _row(jnp.concatenate([mu[:_O_WL], jnp.pad(mu[_O_WL:_O_AL], (0, 96)), jnp.pad(mu[_O_AL:_O_GL], (0, 96)),
                                    jnp.pad(mu[_O_GL:], (0, 32)), mu_vres])),
        'w0': _row(a['rwkv_w0'][l]),
        'w_up': _bf(_pad_rows(a['rwkv_w_up'][l])),
        'a0': _row(a['rwkv_a0'][l]),
        'a_up': _bf(_pad_rows(a['rwkv_a_up'][l])),
        'g_up': _bf(_pad_rows(a['rwkv_g_up'][l])),
        'k_k': _row(a['rwkv_k_k'][l]),
        'k_a': _row(a['rwkv_k_a'][l]),
        'r_k': _row(a['rwkv_r_k'][l]),
        'gn_g': _row(a['rwkv_gn_g'][l]),
        'gn_b': _row(a['rwkv_gn_b'][l]),
        'rwkv_w_o': _bf(a['rwkv_w_o'][l]),
        'sgu_ln_g': _row(a['sgu_ln_g'][l]),
        'sgu_ln_b': _row(a['sgu_ln_b'][l]),
        'sgu_w': jnp.transpose(a['sgu_w'][l], (1, 0, 2)).reshape(SGU_CHUNK, SGU_GROUPS * SGU_CHUNK),
        'sgu_bias': jnp.repeat(jnp.transpose(a['sgu_b'][l]), SGU_W // SGU_GROUPS, axis=1),
        'sgu_w_o': _bf(a['sgu_w_o'][l]),
        'conv_dw': jnp.pad(a['conv_dw'][l], ((0, 1), (0, 0))),
        'conv_db': _row(a['conv_db'][l]),
        'conv_ln_g': _row(a['conv_ln_g'][l]),
        'conv_ln_b': _row(a['conv_ln_b'][l]),
        'conv_w_o': _bf(a['conv_w_o'][l]),
        'w_out': _bf(a['w_out'][l]),
        'ln1_g': _row(a['ln1_g'][l]),
        'ln1_b': _row(a['ln1_b'][l]),
        'ln2_g': _row(a['ln2_g'][l]),
        'ln2_b': _row(a['ln2_b'][l]),
    }
    if l > 0:
        p['v0'] = _row(a['rwkv_v0'][l - 1])
        p['v_up'] = _bf(_pad_rows(a['rwkv_v_up'][l - 1]))
    return p


def _forward(a):
    x = a['x']
    bsz, seq, _ = x.shape
    t = bsz * seq
    x = x.reshape(t, D_MODEL)
    head = jnp.arange(HEAD_SLAB, dtype=jnp.int32) // HEAD_DIM
    head_sum = _bf((head[:, None] == head[None, :]).astype(F32))
    rw = _pad_cols(a['router_w'])
    rw_hi = _bf(rw)
    rw_lo = _bf(rw - rw_hi.astype(F32))
    rw_cat = jnp.concatenate([rw_hi, rw_lo], axis=1)
    rb = jnp.broadcast_to(a['router_b'].astype(F32)[:, None], (N_EXPERTS, 128))
    sb = bsz // STREAMS
    st = sb * seq
    xs = [x[i * st:(i + 1) * st] for i in range(STREAMS)]
    v_first = [None] * STREAMS
    for l in range(DEPTH):
        p = _layer_params(l, a)
        p.update(head_sum=head_sum, router_hi=rw_hi, router_cat=rw_cat, router_b=rb)
        for i in range(STREAMS):
            r, lw, k, v, na, nb, g, ga, part = _mix_in(xs[i], p, v_first[i], sb, seq)
            if l == 0:
                v_first[i] = v
            y = _wkv(r, lw, k, v, na, nb, sb, seq)
            x1, x1b, route, route_t, counts = _mix_out(y, r, k, v, g, ga, part, xs[i], p, sb, seq)
            dest, buf_tok, block_expert, n_used = _dispatch(route_t, counts, st)
            xg = jnp.take(x1b, buf_tok, axis=0, mode='clip')
            yb = _experts(xg, block_expert, n_used, a['moe_w_gate'], a['moe_w_up'], a['moe_w_down'], l)
            y0 = jnp.take(yb, dest[0], axis=0, mode='clip')
            y1 = jnp.take(yb, dest[1], axis=0, mode='clip')
            xs[i] = _combine(x1, y0, y1, route, p['ln2_g'], p['ln2_b'])
    return jnp.concatenate(xs, axis=0).reshape(bsz, seq, D_MODEL)


def kernel(x, w_in_first, w_in_rest, rwkv_mu, rwkv_mu_vres, rwkv_w0, rwkv_w_up, rwkv_a0, rwkv_a_up, rwkv_g_up, rwkv_v0, rwkv_v_up, rwkv_k_k, rwkv_k_a, rwkv_r_k, rwkv_gn_g, rwkv_gn_b, rwkv_w_o, sgu_ln_g, sgu_ln_b, sgu_w, sgu_b, sgu_w_o, conv_dw, conv_db, conv_ln_g, conv_ln_b, conv_w_o, w_out, ln1_g, ln1_b, router_w, router_b, moe_w_gate, moe_w_up, moe_w_down, ln2_g, ln2_b):
    return _forward(dict(
        x=x, w_in_first=w_in_first, w_in_rest=w_in_rest, rwkv_mu=rwkv_mu, rwkv_mu_vres=rwkv_mu_vres,
        rwkv_w0=rwkv_w0, rwkv_w_up=rwkv_w_up, rwkv_a0=rwkv_a0, rwkv_a_up=rwkv_a_up, rwkv_g_up=rwkv_g_up,
        rwkv_v0=rwkv_v0, rwkv_v_up=rwkv_v_up, rwkv_k_k=rwkv_k_k, rwkv_k_a=rwkv_k_a, rwkv_r_k=rwkv_r_k,
        rwkv_gn_g=rwkv_gn_g, rwkv_gn_b=rwkv_gn_b, rwkv_w_o=rwkv_w_o, sgu_ln_g=sgu_ln_g, sgu_ln_b=sgu_ln_b,
        sgu_w=sgu_w, sgu_b=sgu_b, sgu_w_o=sgu_w_o, conv_dw=conv_dw, conv_db=conv_db, conv_ln_g=conv_ln_g,
        conv_ln_b=conv_ln_b, conv_w_o=conv_w_o, w_out=w_out, ln1_g=ln1_g, ln1_b=ln1_b, router_w=router_w,
        router_b=router_b, moe_w_gate=moe_w_gate, moe_w_up=moe_w_up, moe_w_down=moe_w_down, ln2_g=ln2_g,
        ln2_b=ln2_b))
```

```python
import functools

import jax
import jax.numpy as jnp
from jax import lax
from jax.experimental import pallas as pl
from jax.experimental.pallas import tpu as pltpu

F32 = jnp.float32
BF16 = jnp.bfloat16

D_MODEL = 1024
DEPTH = 4
HEADS = 8
HEAD_DIM = 64
RW = HEADS * HEAD_DIM
LORA_PAD = 128
RW_COLS = 3 * RW + 4 * LORA_PAD
SGU_W = 256
SGU_CHUNK = 128
SGU_GROUPS = 4
CONV_W = 256
CONV_K = 31
N_EXPERTS = 32
GROUP_SIZE = 8
N_GROUPS = 4
D_EXPERT = 512
EXPERT_BLOCK = 512
ALPHA = (2 * DEPTH) ** 0.25
LN_EPS = 1e-5
GN_EPS = 64e-5

_O_WL = 3 * RW
_O_AL = _O_WL + 32
_O_GL = _O_AL + 32
_O_SGU = _O_GL + 96
_O_CONV = _O_SGU + 2 * SGU_W
_O_GATE = _O_CONV + 2 * CONV_W
_O_VRES = _O_GATE + 3 * D_MODEL

SEQ_TILE = 256
WKV_CHUNK = 64
WKV_TILE = 256
MIX_OUT_HALVES = 2
COMBINE_TILE = 512
STREAMS = 1
PAIR = 2 * HEAD_DIM
HEAD_SLAB = 256
VMEM_LIMIT = 48 * 1024 * 1024


def _dot(a, b):
    return jnp.dot(a, b, preferred_element_type=F32)


def _dot_nt(a, b):
    return lax.dot_general(a, b, (((1,), (1,)), ((), ())), preferred_element_type=F32)


def _bf(x):
    return x.astype(BF16)


def _head_sum(x_bf16, ones_bd):
    w = ones_bd.shape[0]
    return jnp.concatenate([_dot(x_bf16[:, i:i + w], ones_bd) for i in range(0, x_bf16.shape[1], w)], axis=1)


def _head_sum_split(x, ones_bd):
    hi = _bf(x)
    lo = _bf(x - hi.astype(F32))
    return _head_sum(hi, ones_bd) + _head_sum(lo, ones_bd)


def _split_dot_left(w_bf16, x):
    hi = _bf(x)
    lo = _bf(x - hi.astype(F32))
    return _dot(w_bf16, hi) + _dot(w_bf16, lo)


def _layer_norm(x, g, b):
    mu = jnp.mean(x, axis=-1, keepdims=True)
    d = x - mu
    var = jnp.mean(d * d, axis=-1, keepdims=True)
    return d * lax.rsqrt(var + LN_EPS) * g + b


def _sigmoid(x):
    return 1.0 / (1.0 + jnp.exp(-x))


def _const_spec(shape):
    nd = len(shape)
    return pl.BlockSpec(shape, lambda *_: (0,) * nd, pipeline_mode=pl.Buffered(1))


def _mix_in_body(has_vres, ts, *refs):
    it = iter(refs)
    x_ref, wrw_ref, wsgu_ref, wconv_ref, wgate_ref = (next(it) for _ in range(5))
    mu_ref, w0_ref, wup_ref, a0_ref, aup_ref, gup_ref, kk_ref, ka_ref = (next(it) for _ in range(8))
    if has_vres:
        v0_ref, vup_ref, vfirst_ref = (next(it) for _ in range(3))
    hs_ref = next(it)
    slng_ref, slnb_ref, sw_ref, sbias_ref, swo_ref = (next(it) for _ in range(5))
    cdw_ref, cdb_ref, clng_ref, clnb_ref, cwo_ref = (next(it) for _ in range(5))
    r_o, lw_o, k_o, v_o, a_o, b_o, g_o, ga_o, part_o = (next(it) for _ in range(9))
    pbuf, hext, hshift = next(it), next(it), next(it)

    @pl.when(pl.program_id(1) == 0)
    def _():
        pbuf[0:8, :] = jnp.zeros((8, RW_COLS), F32)
        hext[0:32, :] = jnp.zeros((32, CONV_W), F32)

    xb = _bf(x_ref[...])
    pr = _dot(xb, wrw_ref[...])
    pb = _dot(xb, wsgu_ref[...])
    pc = _dot(xb, wconv_ref[...])
    gate_logits = _dot(xb, wgate_ref[...])

    pbuf[8:8 + ts, :] = pr
    prev = pbuf[pl.ds(7, ts), :]
    pa = pr + (prev - pr) * mu_ref[...]
    pbuf[0:8, :] = pbuf[ts:ts + 8, :]
    r = pa[:, 0:RW]
    k = pa[:, RW:2 * RW]
    v = pa[:, 2 * RW:3 * RW]
    wl = pa[:, 3 * RW:3 * RW + LORA_PAD]
    al = pa[:, 3 * RW + LORA_PAD:3 * RW + 2 * LORA_PAD]
    gl = pa[:, 3 * RW + 2 * LORA_PAD:3 * RW + 3 * LORA_PAD]
    z = w0_ref[...] + _dot(_bf(jnp.tanh(wl)), wup_ref[...])
    nz = -z
    softplus = jnp.maximum(nz, 0.0) + jnp.log(1.0 + jnp.exp(-jnp.abs(nz)))
    lw_o[...] = -jnp.exp(-softplus - 0.5)
    iclr = _sigmoid(a0_ref[...] + _dot(_bf(al), aup_ref[...]))
    g_o[...] = _bf(_dot(_bf(_sigmoid(gl)), gup_ref[...]))
    kkv = k * kk_ref[...]
    ss = _head_sum(_bf(kkv * kkv), hs_ref[...])
    kkn = kkv / jnp.maximum(jnp.sqrt(ss), 1e-12)
    k_o[...] = _bf(k * (1.0 + (iclr - 1.0) * ka_ref[...]))
    if has_vres:
        vl = pa[:, 3 * RW + 3 * LORA_PAD:RW_COLS]
        v = v + (vfirst_ref[...] - v) * _sigmoid(v0_ref[...] + _dot(_bf(vl), vup_ref[...]))
    r_o[...] = _bf(r)
    v_o[...] = _bf(v)
    a_o[...] = _bf(-kkn)
    b_o[...] = _bf(kkn * iclr)

    zz = 0.5 * pb * (1.0 + lax.erf(pb * (2.0 ** -0.5)))
    u = zz[:, :SGU_W]
    vv = _layer_norm(zz[:, SGU_W:], slng_ref[...], slnb_ref[...])
    prow = lax.broadcasted_iota(jnp.int32, (SGU_CHUNK, SGU_GROUPS * SGU_CHUNK), 0)
    qcol = lax.broadcasted_iota(jnp.int32, (SGU_CHUNK, SGU_GROUPS * SGU_CHUNK), 1) % SGU_CHUNK
    wc = _bf(jnp.where(qcol <= prow, sw_ref[...], 0.0))
    lane_grp = lax.broadcasted_iota(jnp.int32, (SGU_CHUNK, SGU_W), 1) // (SGU_W // SGU_GROUPS)
    mixed = []
    for c in range(ts // SGU_CHUNK):
        vc = vv[c * SGU_CHUNK:(c + 1) * SGU_CHUNK]
        stack = jnp.concatenate([jnp.where(lane_grp == g, vc, 0.0) for g in range(SGU_GROUPS)], axis=0)
        mixed.append(_dot(wc, _bf(stack)) + sbias_ref[...])
    mixed = jnp.concatenate(mixed, axis=0)
    o_b = _dot(_bf(u * mixed), swo_ref[...])

    hext[32:32 + ts, :] = pc[:, :CONV_W] * _sigmoid(pc[:, CONV_W:])
    acc = jnp.zeros((ts, CONV_W), F32) + cdb_ref[...]
    first = 32 - (CONV_K - 1)
    for rem in range(8):
        taps = [j for j in range(CONV_K) if (first + j) % 8 == rem]
        span = max(first + j for j in taps) - rem
        hshift[0:span + ts, :] = hext[pl.ds(rem, span + ts), :]
        for j in taps:
            off = first + j - rem
            acc = acc + cdw_ref[j:j + 1, :] * hshift[off:off + ts, :]
    hext[0:32, :] = hext[ts:ts + 32, :]
    hc = _layer_norm(acc, clng_ref[...], clnb_ref[...])
    hc = hc * _sigmoid(hc)
    o_c = _dot(_bf(hc), cwo_ref[...])

    gates = _sigmoid(_bf(gate_logits))
    ga_o[...] = gates[:, :D_MODEL]
    part_o[...] = _bf(gates[:, D_MODEL:2 * D_MODEL] * o_b + gates[:, 2 * D_MODEL:] * o_c)


def _mix_in(x, p, v_first, bsz, seq):
    t = bsz * seq
    ts = SEQ_TILE
    nst = seq // ts
    has_vres = v_first is not None
    row = lambda c: pl.BlockSpec((ts, c), lambda b, s: (b * nst + s, 0))
    args = [x, p['w_rw'], p['w_sgu'], p['w_conv'], p['w_gate'], p['mu'], p['w0'], p['w_up'], p['a0'], p['a_up'],
            p['g_up'], p['k_k'], p['k_a']]
    specs = [row(D_MODEL)] + [_const_spec(a.shape) for a in args[1:]]
    if has_vres:
        args += [p['v0'], p['v_up'], v_first]
        specs += [_const_spec(p['v0'].shape), _const_spec(p['v_up'].shape), row(RW)]
    tail = [p['head_sum'], p['sgu_ln_g'], p['sgu_ln_b'], p['sgu_w'], p['sgu_bias'], p['sgu_w_o'],
            p['conv_dw'], p['conv_db'], p['conv_ln_g'], p['conv_ln_b'], p['conv_w_o']]
    args += tail
    specs += [_const_spec(a.shape) for a in tail]
    out_shape = ([jax.ShapeDtypeStruct((t, RW), F32 if i == 1 else BF16) for i in range(7)]
                 + [jax.ShapeDtypeStruct((t, D_MODEL), BF16)] * 2)
    out_specs = [row(RW)] * 7 + [row(D_MODEL)] * 2
    return pl.pallas_call(
        functools.partial(_mix_in_body, has_vres, ts),
        out_shape=out_shape,
        grid=(bsz, nst),
        in_specs=specs,
        out_specs=out_specs,
        scratch_shapes=[pltpu.VMEM((ts + 8, RW_COLS), F32), pltpu.VMEM((ts + 32, CONV_W), F32),
                        pltpu.VMEM((ts + 32, CONV_W), F32)],
        compiler_params=pltpu.CompilerParams(dimension_semantics=("arbitrary", "arbitrary"),
                                             vmem_limit_bytes=VMEM_LIMIT),
        name="mix_in",
    )(*args)


def _wkv_body(tb, r_ref, lw_ref, k_ref, v_ref, a_ref, b_ref, y_ref, st_ref, wr_s, rs_s, ar_s, vs_s, kb_s, dc_s):
    c = WKV_CHUNK
    nc = tb // c
    npair = HEADS // 2
    units = [(ci, p) for ci in range(nc) for p in range(npair)]

    @pl.when(pl.program_id(1) == 0)
    def _():
        st_ref[...] = jnp.zeros(st_ref.shape, F32)

    lane = lax.broadcasted_iota(jnp.int32, (c, PAIR), 1)
    head0 = lane < HEAD_DIM
    ti = lax.broadcasted_iota(jnp.int32, (PAIR, PAIR), 0)
    si = lax.broadcasted_iota(jnp.int32, (PAIR, PAIR), 1)
    same = (ti // c) == (si // c)
    lower = same & ((si % c) <= (ti % c))
    strict = same & ((si % c) < (ti % c))
    eye = (ti == si).astype(F32)
    tr = lax.broadcasted_iota(jnp.int32, (tb, tb), 0)
    tc = lax.broadcasted_iota(jnp.int32, (tb, tb), 1)
    tri = _bf(((tr // c == tc // c) & (tc <= tr)).astype(F32))

    def stack(x):
        return jnp.concatenate([jnp.where(head0, x, 0.0), jnp.where(head0, 0.0, x)], axis=0)

    lw = lw_ref[...]
    cum = _split_dot_left(tri, lw)
    cum_end = jnp.concatenate(
        [jnp.broadcast_to(cum[(ci + 1) * c - 1:(ci + 1) * c, :], (c, RW)) for ci in range(nc)], axis=0)
    p_inv = jnp.exp(-cum)
    p_end = jnp.exp(cum_end - cum)
    rt = r_ref[...] * jnp.exp(cum)
    at = a_ref[...] * jnp.exp(cum - lw)
    kt = k_ref[...] * p_inv
    bt = b_ref[...] * p_inv
    ke = k_ref[...] * p_end
    be = b_ref[...] * p_end
    vt = v_ref[...]

    def sub(x, ci, p):
        return x[ci * c:(ci + 1) * c, p * PAIR:(p + 1) * PAIR]

    a_s = [stack(sub(at, ci, p)) for ci, p in units]
    r_s = [stack(sub(rt, ci, p)) for ci, p in units]
    g = [_dot_nt(_bf(jnp.concatenate([r_s[i], a_s[i]], axis=0)),
                 _bf(jnp.concatenate([stack(sub(kt, ci, p)), stack(sub(bt, ci, p))], axis=0)))
         for i, (ci, p) in enumerate(units)]
    for i, (ci, p) in enumerate(units):
        vs_s[ci, p] = _bf(stack(sub(vt, ci, p)))
        ar_s[ci, p] = _bf(jnp.concatenate([jnp.where(lower, g[i][0:PAIR, 0:PAIR], 0.0),
                                           jnp.where(lower, g[i][0:PAIR, PAIR:], 0.0)], axis=1))
        kb_s[ci, p] = _bf(jnp.concatenate([stack(sub(ke, ci, p)).T, stack(sub(be, ci, p)).T], axis=1))
        dc_s[ci, p] = jnp.exp(jnp.broadcast_to(sub(cum_end, ci, p)[0:1, :], (PAIR, PAIR))).T
    apow = [jnp.where(strict, g[i][PAIR:, PAIR:], 0.0) for i in range(len(units))]
    inv = [eye + x for x in apow]
    apb = [_bf(x) for x in apow]
    apow = [_dot(x, x) for x in apb]
    for _ in range(4):
        apb = [_bf(x) for x in apow]
        both = [_dot(x, jnp.concatenate([x, _bf(y)], axis=1)) for x, y in zip(apb, inv)]
        apow = [x[:, :PAIR] for x in both]
        inv = [y + x[:, PAIR:] for x, y in zip(both, inv)]
    inv = [y + _dot(_bf(x), _bf(y)) for x, y in zip(apow, inv)]
    for i, (ci, p) in enumerate(units):
        a_ak = jnp.where(strict, g[i][PAIR:, 0:PAIR], 0.0)
        wr_s[ci, p] = _bf(_dot(_bf(inv[i]), _bf(jnp.concatenate([a_s[i], a_ak], axis=1))))
        rs_s[ci, p] = _bf(r_s[i])

    s = [st_ref[p] for p in range(npair)]
    for ci in range(nc):
        sb = [_bf(x) for x in s]
        u = [_dot(wr_s[ci, p], jnp.concatenate([sb[p], vs_s[ci, p]], axis=0)) for p in range(npair)]
        vu = [jnp.concatenate([vs_s[ci, p], _bf(u[p])], axis=0) for p in range(npair)]
        s = [dc_s[ci, p] * s[p] + _dot(kb_s[ci, p], vu[p]) for p in range(npair)]
        for p in range(npair):
            y = _dot(rs_s[ci, p], sb[p]) + _dot(ar_s[ci, p], vu[p])
            y_ref[ci * c:(ci + 1) * c, p * PAIR:(p + 1) * PAIR] = y[0:c] + y[c:]
    for p in range(npair):
        st_ref[p] = s[p]


def _wkv(r, lw, k, v, a, b, bsz, seq):
    t = bsz * seq
    tb = WKV_TILE
    nst = seq // tb
    nc, npair = tb // WKV_CHUNK, HEADS // 2
    row = pl.BlockSpec((tb, RW), lambda bi, s: (bi * nst + s, 0))
    return pl.pallas_call(
        functools.partial(_wkv_body, tb),
        out_shape=jax.ShapeDtypeStruct((t, RW), F32),
        grid=(bsz, nst),
        in_specs=[row] * 6,
        out_specs=row,
        scratch_shapes=[pltpu.VMEM((npair, PAIR, PAIR), F32),
                        pltpu.VMEM((nc, npair, PAIR, 2 * PAIR), BF16),
                        pltpu.VMEM((nc, npair, PAIR, PAIR), BF16),
                        pltpu.VMEM((nc, npair, PAIR, 2 * PAIR), BF16),
                        pltpu.VMEM((nc, npair, PAIR, PAIR), BF16),
                        pltpu.VMEM((nc, npair, PAIR, 2 * PAIR), BF16),
                        pltpu.VMEM((nc, npair, PAIR, PAIR), F32)],
        compiler_params=pltpu.CompilerParams(dimension_semantics=("arbitrary", "arbitrary"),
                                             vmem_limit_bytes=VMEM_LIMIT),
        name="wkv",
    )(r, lw, k, v, a, b)


def _mix_out_body(halves, y_ref, r_ref, k_ref, v_ref, g_ref, ga_ref, part_ref, x_ref, hs_ref, rk_ref, gng_ref,
                  gnb_ref, wo_ref, wout_ref, ln_g_ref, ln_b_ref, rw_hi_ref, rw_cat_ref, rb_ref, x1_o, x1b_o, route_o,
                  route_t_o, counts_o, cnt_ref):
    th = y_ref.shape[0] // halves
    hrows = [pl.ds(h * th, th) for h in range(halves)]

    def each(fn, *cols):
        return [fn(*a) for a in zip(*cols)]

    def load(ref):
        return [ref[rs, :] for rs in hrows]

    hs = hs_ref[...]
    inv_n = 1.0 / HEAD_DIM
    y = load(y_ref)
    mu = each(lambda a: _head_sum_split(a, hs) * inv_n, y)
    d = each(lambda a, m: a - m, y, mu)
    var = each(lambda a: _head_sum(_bf(a * a), hs) * inv_n, d)
    rkk = each(lambda a, b: _head_sum(_bf(a * b * rk_ref[...]), hs), load(r_ref), load(k_ref))
    yn = each(lambda a, s2: a * lax.rsqrt(s2 + GN_EPS) * gng_ref[...] + gnb_ref[...], d, var)
    out = each(lambda a, bo, vv, gg: _bf((a + bo * vv) * gg), yn, rkk, load(v_ref), load(g_ref))
    o_a = each(lambda a: _dot(a, wo_ref[...]), out)
    merged = each(lambda a, b, c2: _bf(a * b + c2), load(ga_ref), o_a, load(part_ref))
    xm = each(lambda a: _dot(a, wout_ref[...]), merged)
    x1 = each(lambda a, b: _layer_norm(ALPHA * a + b, ln_g_ref[...], ln_b_ref[...]), load(x_ref), xm)
    xh = each(_bf, x1)
    for rs, a, b in zip(hrows, x1, xh):
        x1_o[rs, :] = a
        x1b_o[rs, :] = b

    xl = each(lambda a, b: _bf(a - b.astype(F32)), x1, xh)
    hcat = each(lambda a: _dot(a, rw_cat_ref[...]), xh)
    logits = each(lambda c2, b: c2[:, :128] + c2[:, 128:] + _dot(b, rw_hi_ref[...]), hcat, xl)
    lt = each(lambda a: a.T[0:N_EXPERTS, :], logits)
    scores = each(_sigmoid, lt)
    bias = jnp.concatenate([rb_ref[...]] * (th // 128), axis=1)
    sel = each(lambda a: a + bias, scores)
    row = lax.broadcasted_iota(jnp.int32, (N_EXPERTS, th), 0).astype(F32)
    grp_row = lax.broadcasted_iota(jnp.int32, (GROUP_SIZE, th), 0).astype(F32)
    neg = jnp.float32(-jnp.inf)

    def top2(sg, rg):
        m1 = jnp.max(sg, axis=0, keepdims=True)
        i1 = jnp.min(jnp.where(sg == m1, rg, 1e9), axis=0, keepdims=True)
        sg2 = jnp.where(rg == i1, neg, sg)
        m2 = jnp.max(sg2, axis=0, keepdims=True)
        i2 = jnp.min(jnp.where(sg2 == m2, rg, 1e9), axis=0, keepdims=True)
        return m1 + m2, i1, i2

    best = e0 = e1 = None
    for g in range(N_GROUPS):
        grp = slice(g * GROUP_SIZE, (g + 1) * GROUP_SIZE)
        cand = each(lambda a: top2(a[grp], grp_row + float(g * GROUP_SIZE)), sel)
        if g == 0:
            best, e0, e1 = ([c[i] for c in cand] for i in range(3))
        else:
            take = each(lambda c, b: c[0] > b, cand, best)
            best = each(lambda tk, c, b: jnp.where(tk, c[0], b), take, cand, best)
            e0 = each(lambda tk, c, b: jnp.where(tk, c[1], b), take, cand, e0)
            e1 = each(lambda tk, c, b: jnp.where(tk, c[2], b), take, cand, e1)
    g0 = each(lambda e, sc: jnp.sum(jnp.where(row == e, sc, 0.0), axis=0, keepdims=True), e0, scores)
    g1 = each(lambda e, sc: jnp.sum(jnp.where(row == e, sc, 0.0), axis=0, keepdims=True), e1, scores)

    @pl.when(pl.program_id(0) == 0)
    def _():
        cnt_ref[...] = jnp.zeros(cnt_ref.shape, F32)

    chosen = each(lambda a, b: jnp.where((row == a) | (row == b), 1.0, 0.0), e0, e1)
    earlier = _bf((lax.broadcasted_iota(jnp.int32, (th, th), 0)
                   < lax.broadcasted_iota(jnp.int32, (th, th), 1)).astype(F32))
    inside = each(lambda a: _dot(_bf(a), earlier), chosen)
    total = each(lambda a: jnp.sum(a, axis=1, keepdims=True), chosen)
    base = cnt_ref[...]
    row8 = lax.broadcasted_iota(jnp.int32, (8, th), 0)
    for h in range(halves):
        seen = inside[h] + jnp.concatenate([base] * (th // 128), axis=1)
        rank0 = jnp.sum(jnp.where(row == e0[h], seen, 0.0), axis=0, keepdims=True)
        rank1 = jnp.sum(jnp.where(row == e1[h], seen, 0.0), axis=0, keepdims=True)
        tot = g0[h] + g1[h]
        route_t = jnp.where(row8 == 0, e0[h], jnp.where(row8 == 1, e1[h], jnp.where(
            row8 == 2, g0[h] / tot, jnp.where(row8 == 3, g1[h] / tot, jnp.where(
                row8 == 4, rank0, jnp.where(row8 == 5, rank1, 0.0))))))
        route_t_o[:, hrows[h]] = route_t
        route_o[hrows[h], :] = jnp.concatenate([route_t, jnp.zeros((120, th), F32)], axis=0).T
        base = base + total[h]
    cnt_ref[...] = base
    counts_o[...] = base


def _mix_out(y, r, k, v, g, ga, part, x, p, bsz, seq):
    t = bsz * seq
    ts = MIX_OUT_HALVES * SEQ_TILE
    row = lambda c: pl.BlockSpec((ts, c), lambda i: (i, 0))
    consts = [p['head_sum'], p['r_k'], p['gn_g'], p['gn_b'], p['rwkv_w_o'], p['w_out'], p['ln1_g'], p['ln1_b'],
              p['router_hi'], p['router_cat'], p['router_b']]
    return pl.pallas_call(
        functools.partial(_mix_out_body, MIX_OUT_HALVES),
        out_shape=[jax.ShapeDtypeStruct((t, D_MODEL), F32), jax.ShapeDtypeStruct((t, D_MODEL), BF16),
                   jax.ShapeDtypeStruct((t, 128), F32), jax.ShapeDtypeStruct((8, t), F32),
                   jax.ShapeDtypeStruct((N_EXPERTS, 128), F32)],
        grid=(t // ts,),
        in_specs=[row(RW)] * 5 + [row(D_MODEL)] * 3 + [_const_spec(a.shape) for a in consts],
        out_specs=[row(D_MODEL), row(D_MODEL), row(128), pl.BlockSpec((8, ts), lambda i: (0, i)),
                   pl.BlockSpec((N_EXPERTS, 128), lambda i: (0, 0))],
        scratch_shapes=[pltpu.VMEM((N_EXPERTS, 128), F32)],
        compiler_params=pltpu.CompilerParams(dimension_semantics=("arbitrary",), vmem_limit_bytes=VMEM_LIMIT),
        name="mix_out",
    )(y, r, k, v, g, ga, part, x, *consts)


def _experts_body(be_ref, nb_ref, x_ref, wg_ref, wu_ref, wd_ref, o_ref, wg_s, wu_s, wd_s):
    i = pl.program_id(0)
    used = i < nb_ref[0]

    @pl.when(used & ((i == 0) | (be_ref[i] != be_ref[jnp.maximum(i - 1, 0)])))
    def _():
        wg_s[...] = _bf(wg_ref[...])
        wu_s[...] = _bf(wu_ref[...])
        wd_s[...] = _bf(wd_ref[...])

    @pl.when(used)
    def _():
        hb = EXPERT_BLOCK // 2
        rows = [pl.ds(h * hb, hb) for h in range(2)]
        xs = [x_ref[rs, :] for rs in rows]
        hg = [_dot(a, wg_s[...]) for a in xs]
        hu = [_dot(a, wu_s[...]) for a in xs]
        hh = [_bf(a * _sigmoid(a) * b) for a, b in zip(hg, hu)]
        out = [_dot(a, wd_s[...]) for a in hh]
        for rs, a in zip(rows, out):
            o_ref[rs, :] = _bf(a)

    @pl.when(jnp.logical_not(used))
    def _():
        o_ref[...] = jnp.zeros(o_ref.shape, BF16)


def _experts(xg, block_expert, n_used, wg, wu, wd, layer):
    rows = xg.shape[0]
    nb = rows // EXPERT_BLOCK
    grid_spec = pltpu.PrefetchScalarGridSpec(
        num_scalar_prefetch=2,
        grid=(nb,),
        in_specs=[
            pl.BlockSpec((EXPERT_BLOCK, D_MODEL), lambda i, be, nu: (i, 0)),
            pl.BlockSpec((None, None, D_MODEL, D_EXPERT), lambda i, be, nu: (layer, be[i], 0, 0)),
            pl.BlockSpec((None, None, D_MODEL, D_EXPERT), lambda i, be, nu: (layer, be[i], 0, 0)),
            pl.BlockSpec((None, None, D_EXPERT, D_MODEL), lambda i, be, nu: (layer, be[i], 0, 0)),
        ],
        out_specs=pl.BlockSpec((EXPERT_BLOCK, D_MODEL), lambda i, be, nu: (i, 0)),
        scratch_shapes=[pltpu.VMEM((D_MODEL, D_EXPERT), BF16), pltpu.VMEM((D_MODEL, D_EXPERT), BF16),
                        pltpu.VMEM((D_EXPERT, D_MODEL), BF16)],
    )
    return pl.pallas_call(
        _experts_body,
        out_shape=jax.ShapeDtypeStruct((rows, D_MODEL), BF16),
        grid_spec=grid_spec,
        compiler_params=pltpu.CompilerParams(dimension_semantics=("arbitrary",), vmem_limit_bytes=VMEM_LIMIT),
        name="experts",
    )(block_expert, n_used, xg, wg, wu, wd)


def _combine_body(x_ref, y0_ref, y1_ref, route_ref, g_ref, b_ref, o_ref):
    route = route_ref[...]
    moe = route[:, 2:3] * y0_ref[...] + route[:, 3:4] * y1_ref[...]
    o_ref[...] = _layer_norm(ALPHA * x_ref[...] + moe, g_ref[...], b_ref[...])


def _combine(x1, y0, y1, route, ln_g, ln_b):
    t = x1.shape[0]
    ts = COMBINE_TILE
    row = lambda c: pl.BlockSpec((ts, c), lambda i: (i, 0))
    return pl.pallas_call(
        _combine_body,
        out_shape=jax.ShapeDtypeStruct((t, D_MODEL), F32),
        grid=(t // ts,),
        in_specs=[row(D_MODEL)] * 3 + [row(128), _const_spec(ln_g.shape), _const_spec(ln_b.shape)],
        out_specs=row(D_MODEL),
        compiler_params=pltpu.CompilerParams(dimension_semantics=("arbitrary",), vmem_limit_bytes=VMEM_LIMIT),
        name="combine",
    )(x1, y0, y1, route, ln_g, ln_b)


def _dispatch(route_t, counts, t):
    expert = route_t[0:2].astype(jnp.int32)
    rank = route_t[4:6].astype(jnp.int32)
    counts = counts[:, 0].astype(jnp.int32)
    padded = (counts + EXPERT_BLOCK - 1) // EXPERT_BLOCK * EXPERT_BLOCK
    pad_end = jnp.cumsum(padded)
    pad_start = pad_end - padded
    dest = rank
    for e in range(N_EXPERTS):
        dest = dest + jnp.where(expert == e, pad_start[e], 0)
    n_blocks = (2 * t + EXPERT_BLOCK - 1) // EXPERT_BLOCK + N_EXPERTS
    block_start = jnp.arange(n_blocks, dtype=jnp.int32) * EXPERT_BLOCK
    block_expert = jnp.minimum(jnp.sum((pad_end[None, :] <= block_start[:, None]).astype(jnp.int32), axis=1),
                               N_EXPERTS - 1)
    n_used = (pad_end[-1:] // EXPERT_BLOCK).astype(jnp.int32)
    gap = padded - counts
    gap_before = jnp.cumsum(gap) - gap
    n_gap = n_blocks * EXPERT_BLOCK - 2 * t
    g = jnp.arange(n_gap, dtype=jnp.int32)
    gap_row = g
    for e in range(N_EXPERTS):
        gap_row = gap_row + jnp.where(gap_before[e] <= g, counts[e], 0)
    tok = jnp.arange(t, dtype=jnp.int32)
    _, buf_tok = lax.sort((jnp.concatenate([dest.reshape(2 * t), gap_row]),
                           jnp.concatenate([tok, tok, g % t])), num_keys=1)
    return dest, buf_tok, block_expert, n_used


def _pad_cols(w, width=LORA_PAD):
    return jnp.pad(w, ((0, 0), (0, width - w.shape[1])))


def _pad_rows(w, height=LORA_PAD):
    return jnp.pad(w, ((0, height - w.shape[0]), (0, 0)))


def _row(v):
    return v.reshape(1, -1).astype(F32)


def _layer_params(l, a):
    w_in = a['w_in_first'] if l == 0 else a['w_in_rest'][l - 1]
    if l == 0:
        w_vres = jnp.zeros((D_MODEL, LORA_PAD), F32)
        mu_vres = jnp.zeros((LORA_PAD,), F32)
    else:
        w_vres = _pad_cols(w_in[:, _O_VRES:])
        mu_vres = jnp.pad(a['rwkv_mu_vres'][l - 1], (0, LORA_PAD - 32))
    mu = a['rwkv_mu'][l]
    p = {
        'w_rw': _bf(jnp.concatenate([w_in[:, :_O_WL], _pad_cols(w_in[:, _O_WL:_O_AL]), _pad_cols(w_in[:, _O_AL:_O_GL]),
                                     _pad_cols(w_in[:, _O_GL:_O_SGU]), w_vres], axis=1)),
        'w_sgu': _bf(w_in[:, _O_SGU:_O_CONV]),
        'w_conv': _bf(w_in[:, _O_CONV:_O_GATE]),
        'w_gate': _bf(w_in[:, _O_GATE:_O_VRES]),
        'mu': _row(jnp.concatenate([mu[:_O_WL], jnp.pad(mu[_O_WL:_O_AL], (0, 96)), jnp.pad(mu[_O_AL:_O_GL], (0, 96)),
                                    jnp.pad(mu[_O_GL:], (0, 32)), mu_vres])),
        'w0': _row(a['rwkv_w0'][l]),
        'w_up': _bf(_pad_rows(a['rwkv_w_up'][l])),
        'a0': _row(a['rwkv_a0'][l]),
        'a_up': _bf(_pad_rows(a['rwkv_a_up'][l])),
        'g_up': _bf(_pad_rows(a['rwkv_g_up'][l])),
        'k_k': _row(a['rwkv_k_k'][l]),
        'k_a': _row(a['rwkv_k_a'][l]),
        'r_k': _row(a['rwkv_r_k'][l]),
        'gn_g': _row(a['rwkv_gn_g'][l]),
        'gn_b': _row(a['rwkv_gn_b'][l]),
        'rwkv_w_o': _bf(a['rwkv_w_o'][l]),
        'sgu_ln_g': _row(a['sgu_ln_g'][l]),
        'sgu_ln_b': _row(a['sgu_ln_b'][l]),
        'sgu_w': jnp.transpose(a['sgu_w'][l], (1, 0, 2)).reshape(SGU_CHUNK, SGU_GROUPS * SGU_CHUNK),
        'sgu_bias': jnp.repeat(jnp.transpose(a['sgu_b'][l]), SGU_W // SGU_GROUPS, axis=1),
        'sgu_w_o': _bf(a['sgu_w_o'][l]),
        'conv_dw': jnp.pad(a['conv_dw'][l], ((0, 1), (0, 0))),
        'conv_db': _row(a['conv_db'][l]),
        'conv_ln_g': _row(a['conv_ln_g'][l]),
        'conv_ln_b': _row(a['conv_ln_b'][l]),
        'conv_w_o': _bf(a['conv_w_o'][l]),
        'w_out': _bf(a['w_out'][l]),
        'ln1_g': _row(a['ln1_g'][l]),
        'ln1_b': _row(a['ln1_b'][l]),
        'ln2_g': _row(a['ln2_g'][l]),
        'ln2_b': _row(a['ln2_b'][l]),
    }
    if l > 0:
        p['v0'] = _row(a['rwkv_v0'][l - 1])
        p['v_up'] = _bf(_pad_rows(a['rwkv_v_up'][l - 1]))
    return p


def _forward(a):
    x = a['x']
    bsz, seq, _ = x.shape
    t = bsz * seq
    x = x.reshape(t, D_MODEL)
    head = jnp.arange(HEAD_SLAB, dtype=jnp.int32) // HEAD_DIM
    head_sum = _bf((head[:, None] == head[None, :]).astype(F32))
    rw = _pad_cols(a['router_w'])
    rw_hi = _bf(rw)
    rw_lo = _bf(rw - rw_hi.astype(F32))
    rw_cat = jnp.concatenate([rw_hi, rw_lo], axis=1)
    rb = jnp.broadcast_to(a['router_b'].astype(F32)[:, None], (N_EXPERTS, 128))
    sb = bsz // STREAMS
    st = sb * seq
    xs = [x[i * st:(i + 1) * st] for i in range(STREAMS)]
    v_first = [None] * STREAMS
    for l in range(DEPTH):
        p = _layer_params(l, a)
        p.update(head_sum=head_sum, router_hi=rw_hi, router_cat=rw_cat, router_b=rb)
        for i in range(STREAMS):
            r, lw, k, v, na, nb, g, ga, part = _mix_in(xs[i], p, v_first[i], sb, seq)
            if l == 0:
                v_first[i] = v
            y = _wkv(r, lw, k, v, na, nb, sb, seq)
            x1, x1b, route, route_t, counts = _mix_out(y, r, k, v, g, ga, part, xs[i], p, sb, seq)
            dest, buf_tok, block_expert, n_used = _dispatch(route_t, counts, st)
            xg = jnp.take(x1b, buf_tok, axis=0, mode='clip')
            yb = _experts(xg, block_expert, n_used, a['moe_w_gate'], a['moe_w_up'], a['moe_w_down'], l)
            y0 = jnp.take(yb, dest[0], axis=0, mode='clip')
            y1 = jnp.take(yb, dest[1], axis=0, mode='clip')
            xs[i] = _combine(x1, y0, y1, route, p['ln2_g'], p['ln2_b'])
    return jnp.concatenate(xs, axis=0).reshape(bsz, seq, D_MODEL)


def kernel(x, w_in_first, w_in_rest, rwkv_mu, rwkv_mu_vres, rwkv_w0, rwkv_w_up, rwkv_a0, rwkv_a_up, rwkv_g_up, rwkv_v0, rwkv_v_up, rwkv_k_k, rwkv_k_a, rwkv_r_k, rwkv_gn_g, rwkv_gn_b, rwkv_w_o, sgu_ln_g, sgu_ln_b, sgu_w, sgu_b, sgu_w_o, conv_dw, conv_db, conv_ln_g, conv_ln_b, conv_w_o, w_out, ln1_g, ln1_b, router_w, router_b, moe_w_gate, moe_w_up, moe_w_down, ln2_g, ln2_b):
    return _forward(dict(
        x=x, w_in_first=w_in_first, w_in_rest=w_in_rest, rwkv_mu=rwkv_mu, rwkv_mu_vres=rwkv_mu_vres,
        rwkv_w0=rwkv_w0, rwkv_w_up=rwkv_w_up, rwkv_a0=rwkv_a0, rwkv_a_up=rwkv_a_up, rwkv_g_up=rwkv_g_up,
        rwkv_v0=rwkv_v0, rwkv_v_up=rwkv_v_up, rwkv_k_k=rwkv_k_k, rwkv_k_a=rwkv_k_a, rwkv_r_k=rwkv_r_k,
        rwkv_gn_g=rwkv_gn_g, rwkv_gn_b=rwkv_gn_b, rwkv_w_o=rwkv_w_o, sgu_ln_g=sgu_ln_g, sgu_ln_b=sgu_ln_b,
        sgu_w=sgu_w, sgu_b=sgu_b, sgu_w_o=sgu_w_o, conv_dw=conv_dw, conv_db=conv_db, conv_ln_g=conv_ln_g,
        conv_ln_b=conv_ln_b, conv_w_o=conv_w_o, w_out=w_out, ln1_g=ln1_g, ln1_b=ln1_b, router_w=router_w,
        router_b=router_b, moe_w_gate=moe_w_gate, moe_w_up=moe_w_up, moe_w_down=moe_w_down, ln2_g=ln2_g,
        ln2_b=ln2_b))
```

```python
import functools

import jax
import jax.numpy as jnp
from jax import lax
from jax.experimental import pallas as pl
from jax.experimental.pallas import tpu as pltpu

F32 = jnp.float32
BF16 = jnp.bfloat16

D_MODEL = 1024
DEPTH = 4
HEADS = 8
HEAD_DIM = 64
RW = HEADS * HEAD_DIM
LORA_PAD = 128
RW_COLS = 3 * RW + 4 * LORA_PAD
SGU_W = 256
SGU_CHUNK = 128
SGU_GROUPS = 4
CONV_W = 256
CONV_K = 31
N_EXPERTS = 32
GROUP_SIZE = 8
N_GROUPS = 4
D_EXPERT = 512
EXPERT_BLOCK = 512
ALPHA = (2 * DEPTH) ** 0.25
LN_EPS = 1e-5
GN_EPS = 64e-5

_O_WL = 3 * RW
_O_AL = _O_WL + 32
_O_GL = _O_AL + 32
_O_SGU = _O_GL + 96
_O_CONV = _O_SGU + 2 * SGU_W
_O_GATE = _O_CONV + 2 * CONV_W
_O_VRES = _O_GATE + 3 * D_MODEL

SEQ_TILE = 256
WKV_CHUNK = 64
WKV_TILE = 256
MIX_OUT_HALVES = 2
COMBINE_TILE = 512
PAIR = 2 * HEAD_DIM
HEAD_SLAB = 256
VMEM_LIMIT = 48 * 1024 * 1024


def _dot(a, b):
    return jnp.dot(a, b, preferred_element_type=F32)


def _dot_nt(a, b):
    return lax.dot_general(a, b, (((1,), (1,)), ((), ())), preferred_element_type=F32)


def _bf(x):
    return x.astype(BF16)


def _head_sum(x_bf16, ones_bd):
    w = ones_bd.shape[0]
    return jnp.concatenate([_dot(x_bf16[:, i:i + w], ones_bd) for i in range(0, x_bf16.shape[1], w)], axis=1)


def _head_sum_split(x, ones_bd):
    hi = _bf(x)
    lo = _bf(x - hi.astype(F32))
    return _head_sum(hi, ones_bd) + _head_sum(lo, ones_bd)


def _split_dot_left(w_bf16, x):
    hi = _bf(x)
    lo = _bf(x - hi.astype(F32))
    return _dot(w_bf16, hi) + _dot(w_bf16, lo)


def _layer_norm(x, g, b):
    mu = jnp.mean(x, axis=-1, keepdims=True)
    d = x - mu
    var = jnp.mean(d * d, axis=-1, keepdims=True)
    return d * lax.rsqrt(var + LN_EPS) * g + b


def _sigmoid(x):
    return 1.0 / (1.0 + jnp.exp(-x))


def _moe_close(x1, y0, y1, route, g, b):
    return _layer_norm(ALPHA * x1 + route[:, 2:3] * y0 + route[:, 3:4] * y1, g, b)


def _const_spec(shape):
    nd = len(shape)
    return pl.BlockSpec(shape, lambda *_: (0,) * nd, pipeline_mode=pl.Buffered(1))


def _mix_in_body(has_vres, has_moe, ts, *refs):
    it = iter(refs)
    if has_moe:
        x1_ref, y0_ref, y1_ref, route_ref, ln2g_ref, ln2b_ref = (next(it) for _ in range(6))
    else:
        x_ref = next(it)
    wrw_ref, wsgu_ref, wconv_ref, wgate_ref = (next(it) for _ in range(4))
    mu_ref, w0_ref, wup_ref, a0_ref, aup_ref, gup_ref, kk_ref, ka_ref = (next(it) for _ in range(8))
    if has_vres:
        v0_ref, vup_ref, vfirst_ref = (next(it) for _ in range(3))
    hs_ref = next(it)
    slng_ref, slnb_ref, sw_ref, sbias_ref, swo_ref = (next(it) for _ in range(5))
    cdw_ref, cdb_ref, clng_ref, clnb_ref, cwo_ref = (next(it) for _ in range(5))
    r_o, lw_o, k_o, v_o, a_o, b_o, g_o, ga_o, part_o = (next(it) for _ in range(9))
    if has_moe:
        x_o = next(it)
    pbuf, hext, hshift = next(it), next(it), next(it)

    @pl.when(pl.program_id(1) == 0)
    def _():
        pbuf[0:8, :] = jnp.zeros((8, RW_COLS), F32)
        hext[0:32, :] = jnp.zeros((32, CONV_W), F32)

    if has_moe:
        x = _moe_close(x1_ref[...], y0_ref[...], y1_ref[...], route_ref[...], ln2g_ref[...], ln2b_ref[...])
        x_o[...] = x
    else:
        x = x_ref[...]

    xb = _bf(x)
    pr = _dot(xb, wrw_ref[...])
    pb = _dot(xb, wsgu_ref[...])
    pc = _dot(xb, wconv_ref[...])
    gate_logits = _dot(xb, wgate_ref[...])

    pbuf[8:8 + ts, :] = pr
    prev = pbuf[pl.ds(7, ts), :]
    pa = pr + (prev - pr) * mu_ref[...]
    pbuf[0:8, :] = pbuf[ts:ts + 8, :]
    r = pa[:, 0:RW]
    k = pa[:, RW:2 * RW]
    v = pa[:, 2 * RW:3 * RW]
    wl = pa[:, 3 * RW:3 * RW + LORA_PAD]
    al = pa[:, 3 * RW + LORA_PAD:3 * RW + 2 * LORA_PAD]
    gl = pa[:, 3 * RW + 2 * LORA_PAD:3 * RW + 3 * LORA_PAD]
    z = w0_ref[...] + _dot(_bf(jnp.tanh(wl)), wup_ref[...])
    nz = -z
    softplus = jnp.maximum(nz, 0.0) + jnp.log(1.0 + jnp.exp(-jnp.abs(nz)))
    lw_o[...] = -jnp.exp(-softplus - 0.5)
    iclr = _sigmoid(a0_ref[...] + _dot(_bf(al), aup_ref[...]))
    g_o[...] = _bf(_dot(_bf(_sigmoid(gl)), gup_ref[...]))
    kkv = k * kk_ref[...]
    ss = _head_sum(_bf(kkv * kkv), hs_ref[...])
    kkn = kkv / jnp.maximum(jnp.sqrt(ss), 1e-12)
    k_o[...] = _bf(k * (1.0 + (iclr - 1.0) * ka_ref[...]))
    if has_vres:
        vl = pa[:, 3 * RW + 3 * LORA_PAD:RW_COLS]
        v = v + (vfirst_ref[...] - v) * _sigmoid(v0_ref[...] + _dot(_bf(vl), vup_ref[...]))
    r_o[...] = _bf(r)
    v_o[...] = _bf(v)
    a_o[...] = _bf(-kkn)
    b_o[...] = _bf(kkn * iclr)

    zz = 0.5 * pb * (1.0 + lax.erf(pb * (2.0 ** -0.5)))
    u = zz[:, :SGU_W]
    vv = _layer_norm(zz[:, SGU_W:], slng_ref[...], slnb_ref[...])
    prow = lax.broadcasted_iota(jnp.int32, (SGU_CHUNK, SGU_GROUPS * SGU_CHUNK), 0)
    qcol = lax.broadcasted_iota(jnp.int32, (SGU_CHUNK, SGU_GROUPS * SGU_CHUNK), 1) % SGU_CHUNK
    wc = _bf(jnp.where(qcol <= prow, sw_ref[...], 0.0))
    lane_grp = lax.broadcasted_iota(jnp.int32, (SGU_CHUNK, SGU_W), 1) // (SGU_W // SGU_GROUPS)
    mixed = []
    for c in range(ts // SGU_CHUNK):
        vc = vv[c * SGU_CHUNK:(c + 1) * SGU_CHUNK]
        stack = jnp.concatenate([jnp.where(lane_grp == g, vc, 0.0) for g in range(SGU_GROUPS)], axis=0)
        mixed.append(_dot(wc, _bf(stack)) + sbias_ref[...])
    mixed = jnp.concatenate(mixed, axis=0)
    o_b = _dot(_bf(u * mixed), swo_ref[...])

    hext[32:32 + ts, :] = pc[:, :CONV_W] * _sigmoid(pc[:, CONV_W:])
    acc = jnp.zeros((ts, CONV_W), F32) + cdb_ref[...]
    first = 32 - (CONV_K - 1)
    for rem in range(8):
        taps = [j for j in range(CONV_K) if (first + j) % 8 == rem]
        span = max(first + j for j in taps) - rem
        hshift[0:span + ts, :] = hext[pl.ds(rem, span + ts), :]
        for j in taps:
            off = first + j - rem
            acc = acc + cdw_ref[j:j + 1, :] * hshift[off:off + ts, :]
    hext[0:32, :] = hext[ts:ts + 32, :]
    hc = _layer_norm(acc, clng_ref[...], clnb_ref[...])
    hc = hc * _sigmoid(hc)
    o_c = _dot(_bf(hc), cwo_ref[...])

    gates = _sigmoid(_bf(gate_logits))
    ga_o[...] = gates[:, :D_MODEL]
    part_o[...] = _bf(gates[:, D_MODEL:2 * D_MODEL] * o_b + gates[:, 2 * D_MODEL:] * o_c)


def _mix_in(x, p, v_first, bsz, seq, moe=None):
    t = bsz * seq
    ts = SEQ_TILE
    nst = seq // ts
    has_vres = v_first is not None
    has_moe = moe is not None
    row = lambda c: pl.BlockSpec((ts, c), lambda b, s: (b * nst + s, 0))
    if has_moe:
        x1, y0, y1, route, ln2_g, ln2_b = moe
        args = [x1, y0, y1, route, ln2_g, ln2_b]
        specs = [row(D_MODEL)] * 3 + [row(128), _const_spec(ln2_g.shape), _const_spec(ln2_b.shape)]
    else:
        args = [x]
        specs = [row(D_MODEL)]
    consts = [p['w_rw'], p['w_sgu'], p['w_conv'], p['w_gate'], p['mu'], p['w0'], p['w_up'], p['a0'], p['a_up'],
              p['g_up'], p['k_k'], p['k_a']]
    args += consts
    specs += [_const_spec(a.shape) for a in consts]
    if has_vres:
        args += [p['v0'], p['v_up'], v_first]
        specs += [_const_spec(p['v0'].shape), _const_spec(p['v_up'].shape), row(RW)]
    tail = [p['head_sum'], p['sgu_ln_g'], p['sgu_ln_b'], p['sgu_w'], p['sgu_bias'], p['sgu_w_o'],
            p['conv_dw'], p['conv_db'], p['conv_ln_g'], p['conv_ln_b'], p['conv_w_o']]
    args += tail
    specs += [_const_spec(a.shape) for a in tail]
    out_shape = ([jax.ShapeDtypeStruct((t, RW), F32 if i == 1 else BF16) for i in range(7)]
                 + [jax.ShapeDtypeStruct((t, D_MODEL), BF16)] * 2)
    out_specs = [row(RW)] * 7 + [row(D_MODEL)] * 2
    if has_moe:
        out_shape.append(jax.ShapeDtypeStruct((t, D_MODEL), F32))
        out_specs.append(row(D_MODEL))
    return pl.pallas_call(
        functools.partial(_mix_in_body, has_vres, has_moe, ts),
        out_shape=out_shape,
        grid=(bsz, nst),
        in_specs=specs,
        out_specs=out_specs,
        scratch_shapes=[pltpu.VMEM((ts + 8, RW_COLS), F32), pltpu.VMEM((ts + 32, CONV_W), F32),
                        pltpu.VMEM((ts + 32, CONV_W), F32)],
        compiler_params=pltpu.CompilerParams(dimension_semantics=("arbitrary", "arbitrary"),
                                             vmem_limit_bytes=VMEM_LIMIT),
        name="mix_in",
    )(*args)


def _wkv_body(tb, r_ref, lw_ref, k_ref, v_ref, a_ref, b_ref, y_ref, st_ref, wr_s, rs_s, ar_s, vs_s, kb_s, dc_s):
    c = WKV_CHUNK
    nc = tb // c
    npair = HEADS // 2
    units = [(ci, p) for ci in range(nc) for p in range(npair)]

    @pl.when(pl.program_id(1) == 0)
    def _():
        st_ref[...] = jnp.zeros(st_ref.shape, F32)

    lane = lax.broadcasted_iota(jnp.int32, (c, PAIR), 1)
    head0 = lane < HEAD_DIM
    ti = lax.broadcasted_iota(jnp.int32, (PAIR, PAIR), 0)
    si = lax.broadcasted_iota(jnp.int32, (PAIR, PAIR), 1)
    same = (ti // c) == (si // c)
    lower = same & ((si % c) <= (ti % c))
    strict = same & ((si % c) < (ti % c))
    eye = (ti == si).astype(F32)
    tr = lax.broadcasted_iota(jnp.int32, (tb, tb), 0)
    tc = lax.broadcasted_iota(jnp.int32, (tb, tb), 1)
    tri = _bf(((tr // c == tc // c) & (tc <= tr)).astype(F32))

    def stack(x):
        return jnp.concatenate([jnp.where(head0, x, 0.0), jnp.where(head0, 0.0, x)], axis=0)

    lw = lw_ref[...]
    cum = _split_dot_left(tri, lw)
    cum_end = jnp.concatenate(
        [jnp.broadcast_to(cum[(ci + 1) * c - 1:(ci + 1) * c, :], (c, RW)) for ci in range(nc)], axis=0)
    p_inv = jnp.exp(-cum)
    p_end = jnp.exp(cum_end - cum)
    rt = r_ref[...] * jnp.exp(cum)
    at = a_ref[...] * jnp.exp(cum - lw)
    kt = k_ref[...] * p_inv
    bt = b_ref[...] * p_inv
    ke = k_ref[...] * p_end
    be = b_ref[...] * p_end
    vt = v_ref[...]

    def sub(x, ci, p):
        return x[ci * c:(ci + 1) * c, p * PAIR:(p + 1) * PAIR]

    a_s = [stack(sub(at, ci, p)) for ci, p in units]
    r_s = [stack(sub(rt, ci, p)) for ci, p in units]
    g = [_dot_nt(_bf(jnp.concatenate([r_s[i], a_s[i]], axis=0)),
                 _bf(jnp.concatenate([stack(sub(kt, ci, p)), stack(sub(bt, ci, p))], axis=0)))
         for i, (ci, p) in enumerate(units)]
    for i, (ci, p) in enumerate(units):
        vs_s[ci, p] = _bf(stack(sub(vt, ci, p)))
        ar_s[ci, p] = _bf(jnp.concatenate([jnp.where(lower, g[i][0:PAIR, 0:PAIR], 0.0),
                                           jnp.where(lower, g[i][0:PAIR, PAIR:], 0.0)], axis=1))
        kb_s[ci, p] = _bf(jnp.concatenate([stack(sub(ke, ci, p)).T, stack(sub(be, ci, p)).T], axis=1))
        dc_s[ci, p] = jnp.exp(jnp.broadcast_to(sub(cum_end, ci, p)[0:1, :], (PAIR, PAIR))).T
    apow = [jnp.where(strict, g[i][PAIR:, PAIR:], 0.0) for i in range(len(units))]
    inv = [eye + x for x in apow]
    apb = [_bf(x) for x in apow]
    apow = [_dot(x, x) for x in apb]
    for _ in range(4):
        apb = [_bf(x) for x in apow]
        both = [_dot(x, jnp.concatenate([x, _bf(y)], axis=1)) for x, y in zip(apb, inv)]
        apow = [x[:, :PAIR] for x in both]
        inv = [y + x[:, PAIR:] for x, y in zip(both, inv)]
    inv = [y + _dot(_bf(x), _bf(y)) for x, y in zip(apow, inv)]
    for i, (ci, p) in enumerate(units):
        a_ak = jnp.where(strict, g[i][PAIR:, 0:PAIR], 0.0)
        wr_s[ci, p] = _bf(_dot(_bf(inv[i]), _bf(jnp.concatenate([a_s[i], a_ak], axis=1))))
        rs_s[ci, p] = _bf(r_s[i])

    s = [st_ref[p] for p in range(npair)]
    for ci in range(nc):
        sb = [_bf(x) for x in s]
        u = [_dot(wr_s[ci, p], jnp.concatenate([sb[p], vs_s[ci, p]], axis=0)) for p in range(npair)]
        vu = [jnp.concatenate([vs_s[ci, p], _bf(u[p])], axis=0) for p in range(npair)]
        s = [dc_s[ci, p] * s[p] + _dot(kb_s[ci, p], vu[p]) for p in range(npair)]
        for p in range(npair):
            y = _dot(rs_s[ci, p], sb[p]) + _dot(ar_s[ci, p], vu[p])
            y_ref[ci * c:(ci + 1) * c, p * PAIR:(p + 1) * PAIR] = y[0:c] + y[c:]
    for p in range(npair):
        st_ref[p] = s[p]


def _wkv(r, lw, k, v, a, b, bsz, seq):
    t = bsz * seq
    tb = WKV_TILE
    nst = seq // tb
    nc, npair = tb // WKV_CHUNK, HEADS // 2
    row = pl.BlockSpec((tb, RW), lambda bi, s: (bi * nst + s, 0))
    return pl.pallas_call(
        functools.partial(_wkv_body, tb),
        out_shape=jax.ShapeDtypeStruct((t, RW), F32),
        grid=(bsz, nst),
        in_specs=[row] * 6,
        out_specs=row,
        scratch_shapes=[pltpu.VMEM((npair, PAIR, PAIR), F32),
                        pltpu.VMEM((nc, npair, PAIR, 2 * PAIR), BF16),
                        pltpu.VMEM((nc, npair, PAIR, PAIR), BF16),
                        pltpu.VMEM((nc, npair, PAIR, 2 * PAIR), BF16),
                        pltpu.VMEM((nc, npair, PAIR, PAIR), BF16),
                        pltpu.VMEM((nc, npair, PAIR, 2 * PAIR), BF16),
                        pltpu.VMEM((nc, npair, PAIR, PAIR), F32)],
        compiler_params=pltpu.CompilerParams(dimension_semantics=("arbitrary", "arbitrary"),
                                             vmem_limit_bytes=VMEM_LIMIT),
        name="wkv",
    )(r, lw, k, v, a, b)


def _mix_out_body(halves, y_ref, r_ref, k_ref, v_ref, g_ref, ga_ref, part_ref, x_ref, hs_ref, rk_ref, gng_ref,
                  gnb_ref, wo_ref, wout_ref, ln_g_ref, ln_b_ref, rw_hi_ref, rw_cat_ref, rb_ref, x1_o, x1b_o, route_o,
                  route_t_o, counts_o, cnt_ref):
    th = y_ref.shape[0] // halves
    hrows = [pl.ds(h * th, th) for h in range(halves)]

    def each(fn, *cols):
        return [fn(*a) for a in zip(*cols)]

    def load(ref):
        return [ref[rs, :] for rs in hrows]

    hs = hs_ref[...]
    inv_n = 1.0 / HEAD_DIM
    y = load(y_ref)
    mu = each(lambda a: _head_sum_split(a, hs) * inv_n, y)
    d = each(lambda a, m: a - m, y, mu)
    var = each(lambda a: _head_sum(_bf(a * a), hs) * inv_n, d)
    rkk = each(lambda a, b: _head_sum(_bf(a * b * rk_ref[...]), hs), load(r_ref), load(k_ref))
    yn = each(lambda a, s2: a * lax.rsqrt(s2 + GN_EPS) * gng_ref[...] + gnb_ref[...], d, var)
    out = each(lambda a, bo, vv, gg: _bf((a + bo * vv) * gg), yn, rkk, load(v_ref), load(g_ref))
    o_a = each(lambda a: _dot(a, wo_ref[...]), out)
    merged = each(lambda a, b, c2: _bf(a * b + c2), load(ga_ref), o_a, load(part_ref))
    xm = each(lambda a: _dot(a, wout_ref[...]), merged)
    x1 = each(lambda a, b: _layer_norm(ALPHA * a + b, ln_g_ref[...], ln_b_ref[...]), load(x_ref), xm)
    xh = each(_bf, x1)
    for rs, a, b in zip(hrows, x1, xh):
        x1_o[rs, :] = a
        x1b_o[rs, :] = b

    xl = each(lambda a, b: _bf(a - b.astype(F32)), x1, xh)
    hcat = each(lambda a: _dot(a, rw_cat_ref[...]), xh)
    logits = each(lambda c2, b: c2[:, :128] + c2[:, 128:] + _dot(b, rw_hi_ref[...]), hcat, xl)
    lt = each(lambda a: a.T[0:N_EXPERTS, :], logits)
    scores = each(_sigmoid, lt)
    bias = jnp.concatenate([rb_ref[...]] * (th // 128), axis=1)
    sel = each(lambda a: a + bias, scores)
    row = lax.broadcasted_iota(jnp.int32, (N_EXPERTS, th), 0).astype(F32)
    grp_row = lax.broadcasted_iota(jnp.int32, (GROUP_SIZE, th), 0).astype(F32)
    neg = jnp.float32(-jnp.inf)

    def top2(sg, rg):
        m1 = jnp.max(sg, axis=0, keepdims=True)
        i1 = jnp.min(jnp.where(sg == m1, rg, 1e9), axis=0, keepdims=True)
        sg2 = jnp.where(rg == i1, neg, sg)
        m2 = jnp.max(sg2, axis=0, keepdims=True)
        i2 = jnp.min(jnp.where(sg2 == m2, rg, 1e9), axis=0, keepdims=True)
        return m1 + m2, i1, i2

    best = e0 = e1 = None
    for g in range(N_GROUPS):
        grp = slice(g * GROUP_SIZE, (g + 1) * GROUP_SIZE)
        cand = each(lambda a: top2(a[grp], grp_row + float(g * GROUP_SIZE)), sel)
        if g == 0:
            best, e0, e1 = ([c[i] for c in cand] for i in range(3))
        else:
            take = each(lambda c, b: c[0] > b, cand, best)
            best = each(lambda tk, c, b: jnp.where(tk, c[0], b), take, cand, best)
            e0 = each(lambda tk, c, b: jnp.where(tk, c[1], b), take, cand, e0)
            e1 = each(lambda tk, c, b: jnp.where(tk, c[2], b), take, cand, e1)
    g0 = each(lambda e, sc: jnp.sum(jnp.where(row == e, sc, 0.0), axis=0, keepdims=True), e0, scores)
    g1 = each(lambda e, sc: jnp.sum(jnp.where(row == e, sc, 0.0), axis=0, keepdims=True), e1, scores)

    @pl.when(pl.program_id(0) == 0)
    def _():
        cnt_ref[...] = jnp.zeros(cnt_ref.shape, F32)

    chosen = each(lambda a, b: jnp.where((row == a) | (row == b), 1.0, 0.0), e0, e1)
    earlier = _bf((lax.broadcasted_iota(jnp.int32, (th, th), 0)
                   < lax.broadcasted_iota(jnp.int32, (th, th), 1)).astype(F32))
    inside = each(lambda a: _dot(_bf(a), earlier), chosen)
    total = each(lambda a: jnp.sum(a, axis=1, keepdims=True), chosen)
    base = cnt_ref[...]
    row8 = lax.broadcasted_iota(jnp.int32, (8, th), 0)
    for h in range(halves):
        seen = inside[h] + jnp.concatenate([base] * (th // 128), axis=1)
        rank0 = jnp.sum(jnp.where(row == e0[h], seen, 0.0), axis=0, keepdims=True)
        rank1 = jnp.sum(jnp.where(row == e1[h], seen, 0.0), axis=0, keepdims=True)
        tot = g0[h] + g1[h]
        route_t = jnp.where(row8 == 0, e0[h], jnp.where(row8 == 1, e1[h], jnp.where(
            row8 == 2, g0[h] / tot, jnp.where(row8 == 3, g1[h] / tot, jnp.where(
                row8 == 4, rank0, jnp.where(row8 == 5, rank1, 0.0))))))
        route_t_o[:, hrows[h]] = route_t
        route_o[hrows[h], :] = jnp.concatenate([route_t, jnp.zeros((120, th), F32)], axis=0).T
        base = base + total[h]
    cnt_ref[...] = base
    counts_o[...] = base


def _mix_out(y, r, k, v, g, ga, part, x, p, bsz, seq):
    t = bsz * seq
    ts = MIX_OUT_HALVES * SEQ_TILE
    row = lambda c: pl.BlockSpec((ts, c), lambda i: (i, 0))
    consts = [p['head_sum'], p['r_k'], p['gn_g'], p['gn_b'], p['rwkv_w_o'], p['w_out'], p['ln1_g'], p['ln1_b'],
              p['router_hi'], p['router_cat'], p['router_b']]
    return pl.pallas_call(
        functools.partial(_mix_out_body, MIX_OUT_HALVES),
        out_shape=[jax.ShapeDtypeStruct((t, D_MODEL), F32), jax.ShapeDtypeStruct((t, D_MODEL), BF16),
                   jax.ShapeDtypeStruct((t, 128), F32), jax.ShapeDtypeStruct((8, t), F32),
                   jax.ShapeDtypeStruct((N_EXPERTS, 128), F32)],
        grid=(t // ts,),
        in_specs=[row(RW)] * 5 + [row(D_MODEL)] * 3 + [_const_spec(a.shape) for a in consts],
        out_specs=[row(D_MODEL), row(D_MODEL), row(128), pl.BlockSpec((8, ts), lambda i: (0, i)),
                   pl.BlockSpec((N_EXPERTS, 128), lambda i: (0, 0))],
        scratch_shapes=[pltpu.VMEM((N_EXPERTS, 128), F32)],
        compiler_params=pltpu.CompilerParams(dimension_semantics=("arbitrary",), vmem_limit_bytes=VMEM_LIMIT),
        name="mix_out",
    )(y, r, k, v, g, ga, part, x, *consts)


def _experts_body(be_ref, nb_ref, x_ref, wg_ref, wu_ref, wd_ref, o_ref, wg_s, wu_s, wd_s):
    i = pl.program_id(0)
    used = i < nb_ref[0]

    @pl.when(used & ((i == 0) | (be_ref[i] != be_ref[jnp.maximum(i - 1, 0)])))
    def _():
        wg_s[...] = _bf(wg_ref[...])
        wu_s[...] = _bf(wu_ref[...])
        wd_s[...] = _bf(wd_ref[...])

    @pl.when(used)
    def _():
        hb = EXPERT_BLOCK // 2
        rows = [pl.ds(h * hb, hb) for h in range(2)]
        xs = [x_ref[rs, :] for rs in rows]
        hg = [_dot(a, wg_s[...]) for a in xs]
        hu = [_dot(a, wu_s[...]) for a in xs]
        hh = [_bf(a * _sigmoid(a) * b) for a, b in zip(hg, hu)]
        out = [_dot(a, wd_s[...]) for a in hh]
        for rs, a in zip(rows, out):
            o_ref[rs, :] = _bf(a)

    @pl.when(jnp.logical_not(used))
    def _():
        o_ref[...] = jnp.zeros(o_ref.shape, BF16)


def _experts(xg, block_expert, n_used, wg, wu, wd, layer):
    rows = xg.shape[0]
    nb = rows // EXPERT_BLOCK
    grid_spec = pltpu.PrefetchScalarGridSpec(
        num_scalar_prefetch=2,
        grid=(nb,),
        in_specs=[
            pl.BlockSpec((EXPERT_BLOCK, D_MODEL), lambda i, be, nu: (i, 0)),
            pl.BlockSpec((None, None, D_MODEL, D_EXPERT), lambda i, be, nu: (layer, be[i], 0, 0)),
            pl.BlockSpec((None, None, D_MODEL, D_EXPERT), lambda i, be, nu: (layer, be[i], 0, 0)),
            pl.BlockSpec((None, None, D_EXPERT, D_MODEL), lambda i, be, nu: (layer, be[i], 0, 0)),
        ],
        out_specs=pl.BlockSpec((EXPERT_BLOCK, D_MODEL), lambda i, be, nu: (i, 0)),
        scratch_shapes=[pltpu.VMEM((D_MODEL, D_EXPERT), BF16), pltpu.VMEM((D_MODEL, D_EXPERT), BF16),
                        pltpu.VMEM((D_EXPERT, D_MODEL), BF16)],
    )
    return pl.pallas_call(
        _experts_body,
        out_shape=jax.ShapeDtypeStruct((rows, D_MODEL), BF16),
        grid_spec=grid_spec,
        compiler_params=pltpu.CompilerParams(dimension_semantics=("arbitrary",), vmem_limit_bytes=VMEM_LIMIT),
        name="experts",
    )(block_expert, n_used, xg, wg, wu, wd)


def _combine_body(x_ref, y0_ref, y1_ref, route_ref, g_ref, b_ref, o_ref):
    o_ref[...] = _moe_close(x_ref[...], y0_ref[...], y1_ref[...], route_ref[...], g_ref[...], b_ref[...])


def _combine(x1, y0, y1, route, ln_g, ln_b):
    t = x1.shape[0]
    ts = COMBINE_TILE
    row = lambda c: pl.BlockSpec((ts, c), lambda i: (i, 0))
    return pl.pallas_call(
        _combine_body,
        out_shape=jax.ShapeDtypeStruct((t, D_MODEL), F32),
        grid=(t // ts,),
        in_specs=[row(D_MODEL)] * 3 + [row(128), _const_spec(ln_g.shape), _const_spec(ln_b.shape)],
        out_specs=row(D_MODEL),
        compiler_params=pltpu.CompilerParams(dimension_semantics=("arbitrary",), vmem_limit_bytes=VMEM_LIMIT),
        name="combine",
    )(x1, y0, y1, route, ln_g, ln_b)


def _dispatch(route_t, counts, t):
    expert = route_t[0:2].astype(jnp.int32)
    rank = route_t[4:6].astype(jnp.int32)
    counts = counts[:, 0].astype(jnp.int32)
    padded = (counts + EXPERT_BLOCK - 1) // EXPERT_BLOCK * EXPERT_BLOCK
    pad_end = jnp.cumsum(padded)
    pad_start = pad_end - padded
    dest = rank
    for e in range(N_EXPERTS):
        dest = dest + jnp.where(expert == e, pad_start[e], 0)
    n_blocks = (2 * t + EXPERT_BLOCK - 1) // EXPERT_BLOCK + N_EXPERTS
    block_start = jnp.arange(n_blocks, dtype=jnp.int32) * EXPERT_BLOCK
    block_expert = jnp.minimum(jnp.sum((pad_end[None, :] <= block_start[:, None]).astype(jnp.int32), axis=1),
                               N_EXPERTS - 1)
    n_used = (pad_end[-1:] // EXPERT_BLOCK).astype(jnp.int32)
    by_row = jnp.argsort(dest.reshape(2 * t)).astype(jnp.int32) % t
    pad_before = jnp.cumsum(padded - counts) - (padded - counts)
    src = jnp.arange(n_blocks * EXPERT_BLOCK, dtype=jnp.int32) - jnp.repeat(pad_before[block_expert], EXPERT_BLOCK)
    buf_tok = jnp.take(by_row, src % (2 * t), mode='clip')
    return dest, buf_tok, block_expert, n_used


def _pad_cols(w, width=LORA_PAD):
    return jnp.pad(w, ((0, 0), (0, width - w.shape[1])))


def _pad_rows(w, height=LORA_PAD):
    return jnp.pad(w, ((0, height - w.shape[0]), (0, 0)))


def _row(v):
    return v.reshape(1, -1).astype(F32)


def _layer_params(l, a):
    w_in = a['w_in_first'] if l == 0 else a['w_in_rest'][l - 1]
    if l == 0:
        w_vres = jnp.zeros((D_MODEL, LORA_PAD), F32)
        mu_vres = jnp.zeros((LORA_PAD,), F32)
    else:
        w_vres = _pad_cols(w_in[:, _O_VRES:])
        mu_vres = jnp.pad(a['rwkv_mu_vres'][l - 1], (0, LORA_PAD - 32))
    mu = a['rwkv_mu'][l]
    p = {
        'w_rw': _bf(jnp.concatenate([w_in[:, :_O_WL], _pad_cols(w_in[:, _O_WL:_O_AL]), _pad_cols(w_in[:, _O_AL:_O_GL]),
                                     _pad_cols(w_in[:, _O_GL:_O_SGU]), w_vres], axis=1)),
        'w_sgu': _bf(w_in[:, _O_SGU:_O_CONV]),
        'w_conv': _bf(w_in[:, _O_CONV:_O_GATE]),
        'w_gate': _bf(w_in[:, _O_GATE:_O_VRES]),
        'mu': _row(jnp.concatenate([mu[:_O_WL], jnp.pad(mu[_O_WL:_O_AL], (0, 96)), jnp.pad(mu[_O_AL:_O_GL], (0, 96)),
                                    jnp.pad(mu[_O_GL:], (0, 32)), mu_vres])),
        'w0': _row(a['rwkv_w0'][l]),
        'w_up': _bf(_pad_rows(a['rwkv_w_up'][l])),
        'a0': _row(a['rwkv_a0'][l]),
        'a_up': _bf(_pad_rows(a['rwkv_a_up'][l])),
        'g_up': _bf(_pad_rows(a['rwkv_g_up'][l])),
        'k_k': _row(a['rwkv_k_k'][l]),
        'k_a': _row(a['rwkv_k_a'][l]),
        'r_k': _row(a['rwkv_r_k'][l]),
        'gn_g': _row(a['rwkv_gn_g'][l]),
        'gn_b': _row(a['rwkv_gn_b'][l]),
        'rwkv_w_o': _bf(a['rwkv_w_o'][l]),
        'sgu_ln_g': _row(a['sgu_ln_g'][l]),
        'sgu_ln_b': _row(a['sgu_ln_b'][l]),
        'sgu_w': jnp.transpose(a['sgu_w'][l], (1, 0, 2)).reshape(SGU_CHUNK, SGU_GROUPS * SGU_CHUNK),
        'sgu_bias': jnp.repeat(jnp.transpose(a['sgu_b'][l]), SGU_W // SGU_GROUPS, axis=1),
        'sgu_w_o': _bf(a['sgu_w_o'][l]),
        'conv_dw': jnp.pad(a['conv_dw'][l], ((0, 1), (0, 0))),
        'conv_db': _row(a['conv_db'][l]),
        'conv_ln_g': _row(a['conv_ln_g'][l]),
        'conv_ln_b': _row(a['conv_ln_b'][l]),
        'conv_w_o': _bf(a['conv_w_o'][l]),
        'w_out': _bf(a['w_out'][l]),
        'ln1_g': _row(a['ln1_g'][l]),
        'ln1_b': _row(a['ln1_b'][l]),
        'ln2_g': _row(a['ln2_g'][l]),
        'ln2_b': _row(a['ln2_b'][l]),
    }
    if l > 0:
        p['v0'] = _row(a['rwkv_v0'][l - 1])
        p['v_up'] = _bf(_pad_rows(a['rwkv_v_up'][l - 1]))
    return p


def _forward(a):
    x = a['x']
    bsz, seq, _ = x.shape
    t = bsz * seq
    x = x.reshape(t, D_MODEL)
    head = jnp.arange(HEAD_SLAB, dtype=jnp.int32) // HEAD_DIM
    head_sum = _bf((head[:, None] == head[None, :]).astype(F32))
    rw = _pad_cols(a['router_w'])
    rw_hi = _bf(rw)
    rw_lo = _bf(rw - rw_hi.astype(F32))
    rw_cat = jnp.concatenate([rw_hi, rw_lo], axis=1)
    rb = jnp.broadcast_to(a['router_b'].astype(F32)[:, None], (N_EXPERTS, 128))
    v_first = None
    moe = None
    for l in range(DEPTH):
        p = _layer_params(l, a)
        p.update(head_sum=head_sum, router_hi=rw_hi, router_cat=rw_cat, router_b=rb)
        outs = _mix_in(x, p, v_first, bsz, seq, moe)
        r, lw, k, v, na, nb, g, ga, part = outs[:9]
        if l == 0:
            v_first = v
        else:
            x = outs[9]
        y = _wkv(r, lw, k, v, na, nb, bsz, seq)
        x1, x1b, route, route_t, counts = _mix_out(y, r, k, v, g, ga, part, x, p, bsz, seq)
        dest, buf_tok, block_expert, n_used = _dispatch(route_t, counts, t)
        xg = jnp.take(x1b, buf_tok, axis=0, mode='clip')
        yb = _experts(xg, block_expert, n_used, a['moe_w_gate'], a['moe_w_up'], a['moe_w_down'], l)
        y0 = jnp.take(yb, dest[0], axis=0, mode='clip')
        y1 = jnp.take(yb, dest[1], axis=0, mode='clip')
        moe = (x1, y0, y1, route, p['ln2_g'], p['ln2_b'])
    return _combine(*moe).reshape(bsz, seq, D_MODEL)


def kernel(x, w_in_first, w_in_rest, rwkv_mu, rwkv_mu_vres, rwkv_w0, rwkv_w_up, rwkv_a0, rwkv_a_up, rwkv_g_up, rwkv_v0, rwkv_v_up, rwkv_k_k, rwkv_k_a, rwkv_r_k, rwkv_gn_g, rwkv_gn_b, rwkv_w_o, sgu_ln_g, sgu_ln_b, sgu_w, sgu_b, sgu_w_o, conv_dw, conv_db, conv_ln_g, conv_ln_b, conv_w_o, w_out, ln1_g, ln1_b, router_w, router_b, moe_w_gate, moe_w_up, moe_w_down, ln2_g, ln2_b):
    return _forward(dict(
        x=x, w_in_first=w_in_first, w_in_rest=w_in_rest, rwkv_mu=rwkv_mu, rwkv_mu_vres=rwkv_mu_vres,
        rwkv_w0=rwkv_w0, rwkv_w_up=rwkv_w_up, rwkv_a0=rwkv_a0, rwkv_a_up=rwkv_a_up, rwkv_g_up=rwkv_g_up,
        rwkv_v0=rwkv_v0, rwkv_v_up=rwkv_v_up, rwkv_k_k=rwkv_k_k, rwkv_k_a=rwkv_k_a, rwkv_r_k=rwkv_r_k,
        rwkv_gn_g=rwkv_gn_g, rwkv_gn_b=rwkv_gn_b, rwkv_w_o=rwkv_w_o, sgu_ln_g=sgu_ln_g, sgu_ln_b=sgu_ln_b,
        sgu_w=sgu_w, sgu_b=sgu_b, sgu_w_o=sgu_w_o, conv_dw=conv_dw, conv_db=conv_db, conv_ln_g=conv_ln_g,
        conv_ln_b=conv_ln_b, conv_w_o=conv_w_o, w_out=w_out, ln1_g=ln1_g, ln1_b=ln1_b, router_w=router_w,
        router_b=router_b, moe_w_gate=moe_w_gate, moe_w_up=moe_w_up, moe_w_down=moe_w_down, ln2_g=ln2_g,
        ln2_b=ln2_b))
```

```python
import functools

import jax
import jax.numpy as jnp
from jax import lax
from jax.experimental import pallas as pl
from jax.experimental.pallas import tpu as pltpu

F32 = jnp.float32
BF16 = jnp.bfloat16

D_MODEL = 1024
DEPTH = 4
HEADS = 8
HEAD_DIM = 64
RW = HEADS * HEAD_DIM
LORA_PAD = 128
RW_COLS = 3 * RW + 4 * LORA_PAD
SGU_W = 256
SGU_CHUNK = 128
SGU_GROUPS = 4
CONV_W = 256
CONV_K = 31
N_EXPERTS = 32
GROUP_SIZE = 8
N_GROUPS = 4
D_EXPERT = 512
EXPERT_BLOCK = 512
ALPHA = (2 * DEPTH) ** 0.25
LN_EPS = 1e-5
GN_EPS = 64e-5

_O_WL = 3 * RW
_O_AL = _O_WL + 32
_O_GL = _O_AL + 32
_O_SGU = _O_GL + 96
_O_CONV = _O_SGU + 2 * SGU_W
_O_GATE = _O_CONV + 2 * CONV_W
_O_VRES = _O_GATE + 3 * D_MODEL

SEQ_TILE = 256
WKV_CHUNK = 64
WKV_TILE = 256
WKV_SEQS = 2
MIX_OUT_HALVES = 2
COMBINE_TILE = 512
PAIR = 2 * HEAD_DIM
HEAD_SLAB = 256
VMEM_LIMIT = 48 * 1024 * 1024


def _dot(a, b):
    return jnp.dot(a, b, preferred_element_type=F32)


def _dot_nt(a, b):
    return lax.dot_general(a, b, (((1,), (1,)), ((), ())), preferred_element_type=F32)


def _bf(x):
    return x.astype(BF16)


def _head_sum(x_bf16, ones_bd):
    w = ones_bd.shape[0]
    return jnp.concatenate([_dot(x_bf16[:, i:i + w], ones_bd) for i in range(0, x_bf16.shape[1], w)], axis=1)


def _head_sum_split(x, ones_bd):
    hi = _bf(x)
    lo = _bf(x - hi.astype(F32))
    return _head_sum(hi, ones_bd) + _head_sum(lo, ones_bd)


def _split_dot_left(w_bf16, x):
    hi = _bf(x)
    lo = _bf(x - hi.astype(F32))
    return _dot(w_bf16, hi) + _dot(w_bf16, lo)


def _layer_norm(x, g, b):
    mu = jnp.mean(x, axis=-1, keepdims=True)
    d = x - mu
    var = jnp.mean(d * d, axis=-1, keepdims=True)
    return d * lax.rsqrt(var + LN_EPS) * g + b


def _sigmoid(x):
    return 1.0 / (1.0 + jnp.exp(-x))


def _moe_close(x1, y0, y1, route, g, b):
    return _layer_norm(ALPHA * x1 + route[:, 2:3] * y0 + route[:, 3:4] * y1, g, b)


def _const_spec(shape):
    nd = len(shape)
    return pl.BlockSpec(shape, lambda *_: (0,) * nd, pipeline_mode=pl.Buffered(1))


def _mix_in_body(has_vres, has_moe, ts, *refs):
    it = iter(refs)
    if has_moe:
        x1_ref, y0_ref, y1_ref, route_ref, ln2g_ref, ln2b_ref = (next(it) for _ in range(6))
    else:
        x_ref = next(it)
    wrw_ref, wsgu_ref, wconv_ref, wgate_ref = (next(it) for _ in range(4))
    mu_ref, w0_ref, wup_ref, a0_ref, aup_ref, gup_ref, kk_ref, ka_ref = (next(it) for _ in range(8))
    if has_vres:
        v0_ref, vup_ref, vfirst_ref = (next(it) for _ in range(3))
    hs_ref = next(it)
    slng_ref, slnb_ref, sw_ref, sbias_ref, swo_ref = (next(it) for _ in range(5))
    cdw_ref, cdb_ref, clng_ref, clnb_ref, cwo_ref = (next(it) for _ in range(5))
    r_o, lw_o, k_o, v_o, a_o, b_o, g_o, ga_o, part_o = (next(it) for _ in range(9))
    if has_moe:
        x_o = next(it)
    pbuf, hext, hshift = next(it), next(it), next(it)

    @pl.when(pl.program_id(1) == 0)
    def _():
        pbuf[0:8, :] = jnp.zeros((8, RW_COLS), F32)
        hext[0:32, :] = jnp.zeros((32, CONV_W), F32)

    if has_moe:
        x = _moe_close(x1_ref[...], y0_ref[...], y1_ref[...], route_ref[...], ln2g_ref[...], ln2b_ref[...])
        x_o[...] = x
    else:
        x = x_ref[...]

    xb = _bf(x)
    pr = _dot(xb, wrw_ref[...])
    pb = _dot(xb, wsgu_ref[...])
    pc = _dot(xb, wconv_ref[...])
    gate_logits = _dot(xb, wgate_ref[...])

    pbuf[8:8 + ts, :] = pr
    prev = pbuf[pl.ds(7, ts), :]
    pa = pr + (prev - pr) * mu_ref[...]
    pbuf[0:8, :] = pbuf[ts:ts + 8, :]
    r = pa[:, 0:RW]
    k = pa[:, RW:2 * RW]
    v = pa[:, 2 * RW:3 * RW]
    wl = pa[:, 3 * RW:3 * RW + LORA_PAD]
    al = pa[:, 3 * RW + LORA_PAD:3 * RW + 2 * LORA_PAD]
    gl = pa[:, 3 * RW + 2 * LORA_PAD:3 * RW + 3 * LORA_PAD]
    z = w0_ref[...] + _dot(_bf(jnp.tanh(wl)), wup_ref[...])
    nz = -z
    softplus = jnp.maximum(nz, 0.0) + jnp.log(1.0 + jnp.exp(-jnp.abs(nz)))
    lw_o[...] = -jnp.exp(-softplus - 0.5)
    iclr = _sigmoid(a0_ref[...] + _dot(_bf(al), aup_ref[...]))
    g_o[...] = _bf(_dot(_bf(_sigmoid(gl)), gup_ref[...]))
    kkv = k * kk_ref[...]
    ss = _head_sum(_bf(kkv * kkv), hs_ref[...])
    kkn = kkv / jnp.maximum(jnp.sqrt(ss), 1e-12)
    k_o[...] = _bf(k * (1.0 + (iclr - 1.0) * ka_ref[...]))
    if has_vres:
        vl = pa[:, 3 * RW + 3 * LORA_PAD:RW_COLS]
        v = v + (vfirst_ref[...] - v) * _sigmoid(v0_ref[...] + _dot(_bf(vl), vup_ref[...]))
    r_o[...] = _bf(r)
    v_o[...] = _bf(v)
    a_o[...] = _bf(-kkn)
    b_o[...] = _bf(kkn * iclr)

    zz = 0.5 * pb * (1.0 + lax.erf(pb * (2.0 ** -0.5)))
    u = zz[:, :SGU_W]
    vv = _layer_norm(zz[:, SGU_W:], slng_ref[...], slnb_ref[...])
    prow = lax.broadcasted_iota(jnp.int32, (SGU_CHUNK, SGU_GROUPS * SGU_CHUNK), 0)
    qcol = lax.broadcasted_iota(jnp.int32, (SGU_CHUNK, SGU_GROUPS * SGU_CHUNK), 1) % SGU_CHUNK
    wc = _bf(jnp.where(qcol <= prow, sw_ref[...], 0.0))
    lane_grp = lax.broadcasted_iota(jnp.int32, (SGU_CHUNK, SGU_W), 1) // (SGU_W // SGU_GROUPS)
    mixed = []
    for c in range(ts // SGU_CHUNK):
        vc = vv[c * SGU_CHUNK:(c + 1) * SGU_CHUNK]
        stack = jnp.concatenate([jnp.where(lane_grp == g, vc, 0.0) for g in range(SGU_GROUPS)], axis=0)
        mixed.append(_dot(wc, _bf(stack)) + sbias_ref[...])
    mixed = jnp.concatenate(mixed, axis=0)
    o_b = _dot(_bf(u * mixed), swo_ref[...])

    hext[32:32 + ts, :] = pc[:, :CONV_W] * _sigmoid(pc[:, CONV_W:])
    acc = jnp.zeros((ts, CONV_W), F32) + cdb_ref[...]
    first = 32 - (CONV_K - 1)
    for rem in range(8):
        taps = [j for j in range(CONV_K) if (first + j) % 8 == rem]
        span = max(first + j for j in taps) - rem
        hshift[0:span + ts, :] = hext[pl.ds(rem, span + ts), :]
        for j in taps:
            off = first + j - rem
            acc = acc + cdw_ref[j:j + 1, :] * hshift[off:off + ts, :]
    hext[0:32, :] = hext[ts:ts + 32, :]
    hc = _layer_norm(acc, clng_ref[...], clnb_ref[...])
    hc = hc * _sigmoid(hc)
    o_c = _dot(_bf(hc), cwo_ref[...])

    gates = _sigmoid(_bf(gate_logits))
    ga_o[...] = gates[:, :D_MODEL]
    part_o[...] = _bf(gates[:, D_MODEL:2 * D_MODEL] * o_b + gates[:, 2 * D_MODEL:] * o_c)


def _mix_in(x, p, v_first, bsz, seq, moe=None):
    t = bsz * seq
    ts = SEQ_TILE
    nst = seq // ts
    has_vres = v_first is not None
    has_moe = moe is not None
    row = lambda c: pl.BlockSpec((ts, c), lambda b, s: (b * nst + s, 0))
    if has_moe:
        x1, y0, y1, route, ln2_g, ln2_b = moe
        args = [x1, y0, y1, route, ln2_g, ln2_b]
        specs = [row(D_MODEL)] * 3 + [row(128), _const_spec(ln2_g.shape), _const_spec(ln2_b.shape)]
    else:
        args = [x]
        specs = [row(D_MODEL)]
    consts = [p['w_rw'], p['w_sgu'], p['w_conv'], p['w_gate'], p['mu'], p['w0'], p['w_up'], p['a0'], p['a_up'],
              p['g_up'], p['k_k'], p['k_a']]
    args += consts
    specs += [_const_spec(a.shape) for a in consts]
    if has_vres:
        args += [p['v0'], p['v_up'], v_first]
        specs += [_const_spec(p['v0'].shape), _const_spec(p['v_up'].shape), row(RW)]
    tail = [p['head_sum'], p['sgu_ln_g'], p['sgu_ln_b'], p['sgu_w'], p['sgu_bias'], p['sgu_w_o'],
            p['conv_dw'], p['conv_db'], p['conv_ln_g'], p['conv_ln_b'], p['conv_w_o']]
    args += tail
    specs += [_const_spec(a.shape) for a in tail]
    out_shape = ([jax.ShapeDtypeStruct((t, RW), F32 if i == 1 else BF16) for i in range(7)]
                 + [jax.ShapeDtypeStruct((t, D_MODEL), BF16)] * 2)
    out_specs = [row(RW)] * 7 + [row(D_MODEL)] * 2
    if has_moe:
        out_shape.append(jax.ShapeDtypeStruct((t, D_MODEL), F32))
        out_specs.append(row(D_MODEL))
    return pl.pallas_call(
        functools.partial(_mix_in_body, has_vres, has_moe, ts),
        out_shape=out_shape,
        grid=(bsz, nst),
        in_specs=specs,
        out_specs=out_specs,
        scratch_shapes=[pltpu.VMEM((ts + 8, RW_COLS), F32), pltpu.VMEM((ts + 32, CONV_W), F32),
                        pltpu.VMEM((ts + 32, CONV_W), F32)],
        compiler_params=pltpu.CompilerParams(dimension_semantics=("arbitrary", "arbitrary"),
                                             vmem_limit_bytes=VMEM_LIMIT),
        name="mix_in",
    )(*args)


def _wkv_body(tb, nseq, r_ref, lw_ref, k_ref, v_ref, a_ref, b_ref, y_ref, st_ref, wr_s, rs_s, ar_s, vs_s, kb_s, dc_s):
    c = WKV_CHUNK
    nc = tb // c
    npair = HEADS // 2
    nct = nseq * nc
    units = [(ci, p) for ci in range(nct) for p in range(npair)]

    @pl.when(pl.program_id(1) == 0)
    def _():
        st_ref[...] = jnp.zeros(st_ref.shape, F32)

    lane = lax.broadcasted_iota(jnp.int32, (c, PAIR), 1)
    head0 = lane < HEAD_DIM
    ti = lax.broadcasted_iota(jnp.int32, (PAIR, PAIR), 0)
    si = lax.broadcasted_iota(jnp.int32, (PAIR, PAIR), 1)
    same = (ti // c) == (si // c)
    lower = same & ((si % c) <= (ti % c))
    strict = same & ((si % c) < (ti % c))
    eye = (ti == si).astype(F32)
    tr = lax.broadcasted_iota(jnp.int32, (tb, tb), 0)
    tc = lax.broadcasted_iota(jnp.int32, (tb, tb), 1)
    tri = _bf(((tr // c == tc // c) & (tc <= tr)).astype(F32))

    def stack(x):
        return jnp.concatenate([jnp.where(head0, x, 0.0), jnp.where(head0, 0.0, x)], axis=0)

    def rows(ref):
        return ref[...].reshape(nseq * tb, RW)

    lw = rows(lw_ref)
    cum = jnp.concatenate([_split_dot_left(tri, lw[q * tb:(q + 1) * tb]) for q in range(nseq)], axis=0)
    cum_end = jnp.concatenate(
        [jnp.broadcast_to(cum[(ci + 1) * c - 1:(ci + 1) * c, :], (c, RW)) for ci in range(nct)], axis=0)
    p_inv = jnp.exp(-cum)
    p_end = jnp.exp(cum_end - cum)
    rt = rows(r_ref) * jnp.exp(cum)
    at = rows(a_ref) * jnp.exp(cum - lw)
    kt = rows(k_ref) * p_inv
    bt = rows(b_ref) * p_inv
    ke = rows(k_ref) * p_end
    be = rows(b_ref) * p_end
    vt = rows(v_ref)

    def sub(x, ci, p):
        return x[ci * c:(ci + 1) * c, p * PAIR:(p + 1) * PAIR]

    a_s = [stack(sub(at, ci, p)) for ci, p in units]
    r_s = [stack(sub(rt, ci, p)) for ci, p in units]
    g = [_dot_nt(_bf(jnp.concatenate([r_s[i], a_s[i]], axis=0)),
                 _bf(jnp.concatenate([stack(sub(kt, ci, p)), stack(sub(bt, ci, p))], axis=0)))
         for i, (ci, p) in enumerate(units)]
    for i, (ci, p) in enumerate(units):
        vs_s[ci, p] = _bf(stack(sub(vt, ci, p)))
        ar_s[ci, p] = _bf(jnp.concatenate([jnp.where(lower, g[i][0:PAIR, 0:PAIR], 0.0),
                                           jnp.where(lower, g[i][0:PAIR, PAIR:], 0.0)], axis=1))
        kb_s[ci, p] = _bf(jnp.concatenate([stack(sub(ke, ci, p)).T, stack(sub(be, ci, p)).T], axis=1))
        dc_s[ci, p] = jnp.exp(jnp.broadcast_to(sub(cum_end, ci, p)[0:1, :], (PAIR, PAIR))).T
    apow = [jnp.where(strict, g[i][PAIR:, PAIR:], 0.0) for i in range(len(units))]
    inv = [eye + x for x in apow]
    apb = [_bf(x) for x in apow]
    apow = [_dot(x, x) for x in apb]
    for _ in range(4):
        apb = [_bf(x) for x in apow]
        both = [_dot(x, jnp.concatenate([x, _bf(y)], axis=1)) for x, y in zip(apb, inv)]
        apow = [x[:, :PAIR] for x in both]
        inv = [y + x[:, PAIR:] for x, y in zip(both, inv)]
    inv = [y + _dot(_bf(x), _bf(y)) for x, y in zip(apow, inv)]
    for i, (ci, p) in enumerate(units):
        a_ak = jnp.where(strict, g[i][PAIR:, 0:PAIR], 0.0)
        wr_s[ci, p] = _bf(_dot(_bf(inv[i]), _bf(jnp.concatenate([a_s[i], a_ak], axis=1))))
        rs_s[ci, p] = _bf(r_s[i])

    chains = [(q, p) for q in range(nseq) for p in range(npair)]
    s = [st_ref[q, p] for q, p in chains]
    for ci in range(nc):
        cu = [q * nc + ci for q, _ in chains]
        sb = [_bf(x) for x in s]
        u = [_dot(wr_s[cu[j], p], jnp.concatenate([sb[j], vs_s[cu[j], p]], axis=0)) for j, (q, p) in enumerate(chains)]
        vu = [jnp.concatenate([vs_s[cu[j], p], _bf(u[j])], axis=0) for j, (q, p) in enumerate(chains)]
        s = [dc_s[cu[j], p] * s[j] + _dot(kb_s[cu[j], p], vu[j]) for j, (q, p) in enumerate(chains)]
        for j, (q, p) in enumerate(chains):
            y = _dot(rs_s[cu[j], p], sb[j]) + _dot(ar_s[cu[j], p], vu[j])
            y_ref[q, ci * c:(ci + 1) * c, p * PAIR:(p + 1) * PAIR] = y[0:c] + y[c:]
    for j, (q, p) in enumerate(chains):
        st_ref[q, p] = s[j]


def _wkv(r, lw, k, v, a, b, bsz, seq):
    t = bsz * seq
    tb = WKV_TILE
    nst = seq // tb
    nq = WKV_SEQS
    nc, npair = nq * (tb // WKV_CHUNK), HEADS // 2
    blk = pl.BlockSpec((nq, tb, RW), lambda bi, s: (bi, s, 0))
    view = lambda x: x.reshape(bsz, seq, RW)
    return pl.pallas_call(
        functools.partial(_wkv_body, tb, nq),
        out_shape=jax.ShapeDtypeStruct((bsz, seq, RW), F32),
        grid=(bsz // nq, nst),
        in_specs=[blk] * 6,
        out_specs=blk,
        scratch_shapes=[pltpu.VMEM((nq, npair, PAIR, PAIR), F32),
                        pltpu.VMEM((nc, npair, PAIR, 2 * PAIR), BF16),
                        pltpu.VMEM((nc, npair, PAIR, PAIR), BF16),
                        pltpu.VMEM((nc, npair, PAIR, 2 * PAIR), BF16),
                        pltpu.VMEM((nc, npair, PAIR, PAIR), BF16),
                        pltpu.VMEM((nc, npair, PAIR, 2 * PAIR), BF16),
                        pltpu.VMEM((nc, npair, PAIR, PAIR), F32)],
        compiler_params=pltpu.CompilerParams(dimension_semantics=("arbitrary", "arbitrary"),
                                             vmem_limit_bytes=VMEM_LIMIT),
        name="wkv",
    )(*(view(x) for x in (r, lw, k, v, a, b))).reshape(t, RW)


def _mix_out_body(halves, y_ref, r_ref, k_ref, v_ref, g_ref, ga_ref, part_ref, x_ref, hs_ref, rk_ref, gng_ref,
                  gnb_ref, wo_ref, wout_ref, ln_g_ref, ln_b_ref, rw_hi_ref, rw_cat_ref, rb_ref, x1_o, x1b_o, route_o,
                  route_t_o, counts_o, cnt_ref):
    th = y_ref.shape[0] // halves
    hrows = [pl.ds(h * th, th) for h in range(halves)]

    def each(fn, *cols):
        return [fn(*a) for a in zip(*cols)]

    def load(ref):
        return [ref[rs, :] for rs in hrows]

    hs = hs_ref[...]
    inv_n = 1.0 / HEAD_DIM
    y = load(y_ref)
    mu = each(lambda a: _head_sum_split(a, hs) * inv_n, y)
    d = each(lambda a, m: a - m, y, mu)
    var = each(lambda a: _head_sum(_bf(a * a), hs) * inv_n, d)
    rkk = each(lambda a, b: _head_sum(_bf(a * b * rk_ref[...]), hs), load(r_ref), load(k_ref))
    yn = each(lambda a, s2: a * lax.rsqrt(s2 + GN_EPS) * gng_ref[...] + gnb_ref[...], d, var)
    out = each(lambda a, bo, vv, gg: _bf((a + bo * vv) * gg), yn, rkk, load(v_ref), load(g_ref))
    o_a = each(lambda a: _dot(a, wo_ref[...]), out)
    merged = each(lambda a, b, c2: _bf(a * b + c2), load(ga_ref), o_a, load(part_ref))
    xm = each(lambda a: _dot(a, wout_ref[...]), merged)
    x1 = each(lambda a, b: _layer_norm(ALPHA * a + b, ln_g_ref[...], ln_b_ref[...]), load(x_ref), xm)
    xh = each(_bf, x1)
    for rs, a, b in zip(hrows, x1, xh):
        x1_o[rs, :] = a
        x1b_o[rs, :] = b

    xl = each(lambda a, b: _bf(a - b.astype(F32)), x1, xh)
    hcat = each(lambda a: _dot(a, rw_cat_ref[...]), xh)
    logits = each(lambda c2, b: c2[:, :128] + c2[:, 128:] + _dot(b, rw_hi_ref[...]), hcat, xl)
    lt = each(lambda a: a.T[0:N_EXPERTS, :], logits)
    scores = each(_sigmoid, lt)
    bias = jnp.concatenate([rb_ref[...]] * (th // 128), axis=1)
    sel = each(lambda a: a + bias, scores)
    row = lax.broadcasted_iota(jnp.int32, (N_EXPERTS, th), 0).astype(F32)
    grp_row = lax.broadcasted_iota(jnp.int32, (GROUP_SIZE, th), 0).astype(F32)
    neg = jnp.float32(-jnp.inf)

    def top2(sg, rg):
        m1 = jnp.max(sg, axis=0, keepdims=True)
        i1 = jnp.min(jnp.where(sg == m1, rg, 1e9), axis=0, keepdims=True)
        sg2 = jnp.where(rg == i1, neg, sg)
        m2 = jnp.max(sg2, axis=0, keepdims=True)
        i2 = jnp.min(jnp.where(sg2 == m2, rg, 1e9), axis=0, keepdims=True)
        return m1 + m2, i1, i2

    best = e0 = e1 = None
    for g in range(N_GROUPS):
        grp = slice(g * GROUP_SIZE, (g + 1) * GROUP_SIZE)
        cand = each(lambda a: top2(a[grp], grp_row + float(g * GROUP_SIZE)), sel)
        if g == 0:
            best, e0, e1 = ([c[i] for c in cand] for i in range(3))
        else:
            take = each(lambda c, b: c[0] > b, cand, best)
            best = each(lambda tk, c, b: jnp.where(tk, c[0], b), take, cand, best)
            e0 = each(lambda tk, c, b: jnp.where(tk, c[1], b), take, cand, e0)
            e1 = each(lambda tk, c, b: jnp.where(tk, c[2], b), take, cand, e1)
    g0 = each(lambda e, sc: jnp.sum(jnp.where(row == e, sc, 0.0), axis=0, keepdims=True), e0, scores)
    g1 = each(lambda e, sc: jnp.sum(jnp.where(row == e, sc, 0.0), axis=0, keepdims=True), e1, scores)

    @pl.when(pl.program_id(0) == 0)
    def _():
        cnt_ref[...] = jnp.zeros(cnt_ref.shape, F32)

    chosen = each(lambda a, b: jnp.where((row == a) | (row == b), 1.0, 0.0), e0, e1)
    earlier = _bf((lax.broadcasted_iota(jnp.int32, (th, th), 0)
                   < lax.broadcasted_iota(jnp.int32, (th, th), 1)).astype(F32))
    inside = each(lambda a: _dot(_bf(a), earlier), chosen)
    total = each(lambda a: jnp.sum(a, axis=1, keepdims=True), chosen)
    base = cnt_ref[...]
    row8 = lax.broadcasted_iota(jnp.int32, (8, th), 0)
    for h in range(halves):
        seen = inside[h] + jnp.concatenate([base] * (th // 128), axis=1)
        rank0 = jnp.sum(jnp.where(row == e0[h], seen, 0.0), axis=0, keepdims=True)
        rank1 = jnp.sum(jnp.where(row == e1[h], seen, 0.0), axis=0, keepdims=True)
        tot = g0[h] + g1[h]
        route_t = jnp.where(row8 == 0, e0[h], jnp.where(row8 == 1, e1[h], jnp.where(
            row8 == 2, g0[h] / tot, jnp.where(row8 == 3, g1[h] / tot, jnp.where(
                row8 == 4, rank0, jnp.where(row8 == 5, rank1, 0.0))))))
        route_t_o[:, hrows[h]] = route_t
        route_o[hrows[h], :] = jnp.concatenate([route_t, jnp.zeros((120, th), F32)], axis=0).T
        base = base + total[h]
    cnt_ref[...] = base
    counts_o[...] = base


def _mix_out(y, r, k, v, g, ga, part, x, p, bsz, seq):
    t = bsz * seq
    ts = MIX_OUT_HALVES * SEQ_TILE
    row = lambda c: pl.BlockSpec((ts, c), lambda i: (i, 0))
    consts = [p['head_sum'], p['r_k'], p['gn_g'], p['gn_b'], p['rwkv_w_o'], p['w_out'], p['ln1_g'], p['ln1_b'],
              p['router_hi'], p['router_cat'], p['router_b']]
    return pl.pallas_call(
        functools.partial(_mix_out_body, MIX_OUT_HALVES),
        out_shape=[jax.ShapeDtypeStruct((t, D_MODEL), F32), jax.ShapeDtypeStruct((t, D_MODEL), BF16),
                   jax.ShapeDtypeStruct((t, 128), F32), jax.ShapeDtypeStruct((8, t), F32),
                   jax.ShapeDtypeStruct((N_EXPERTS, 128), F32)],
        grid=(t // ts,),
        in_specs=[row(RW)] * 5 + [row(D_MODEL)] * 3 + [_const_spec(a.shape) for a in consts],
        out_specs=[row(D_MODEL), row(D_MODEL), row(128), pl.BlockSpec((8, ts), lambda i: (0, i)),
                   pl.BlockSpec((N_EXPERTS, 128), lambda i: (0, 0))],
        scratch_shapes=[pltpu.VMEM((N_EXPERTS, 128), F32)],
        compiler_params=pltpu.CompilerParams(dimension_semantics=("arbitrary",), vmem_limit_bytes=VMEM_LIMIT),
        name="mix_out",
    )(y, r, k, v, g, ga, part, x, *consts)


def _experts_body(be_ref, nb_ref, x_ref, wg_ref, wu_ref, wd_ref, o_ref, wg_s, wu_s, wd_s):
    i = pl.program_id(0)
    used = i < nb_ref[0]

    @pl.when(used & ((i == 0) | (be_ref[i] != be_ref[jnp.maximum(i - 1, 0)])))
    def _():
        wg_s[...] = _bf(wg_ref[...])
        wu_s[...] = _bf(wu_ref[...])
        wd_s[...] = _bf(wd_ref[...])

    @pl.when(used)
    def _():
        hb = EXPERT_BLOCK // 2
        rows = [pl.ds(h * hb, hb) for h in range(2)]
        xs = [x_ref[rs, :] for rs in rows]
        hg = [_dot(a, wg_s[...]) for a in xs]
        hu = [_dot(a, wu_s[...]) for a in xs]
        hh = [_bf(a * _sigmoid(a) * b) for a, b in zip(hg, hu)]
        out = [_dot(a, wd_s[...]) for a in hh]
        for rs, a in zip(rows, out):
            o_ref[rs, :] = _bf(a)

    @pl.when(jnp.logical_not(used))
    def _():
        o_ref[...] = jnp.zeros(o_ref.shape, BF16)


def _experts(xg, block_expert, n_used, wg, wu, wd, layer):
    rows = xg.shape[0]
    nb = rows // EXPERT_BLOCK
    grid_spec = pltpu.PrefetchScalarGridSpec(
        num_scalar_prefetch=2,
        grid=(nb,),
        in_specs=[
            pl.BlockSpec((EXPERT_BLOCK, D_MODEL), lambda i, be, nu: (i, 0)),
            pl.BlockSpec((None, None, D_MODEL, D_EXPERT), lambda i, be, nu: (layer, be[i], 0, 0)),
            pl.BlockSpec((None, None, D_MODEL, D_EXPERT), lambda i, be, nu: (layer, be[i], 0, 0)),
            pl.BlockSpec((None, None, D_EXPERT, D_MODEL), lambda i, be, nu: (layer, be[i], 0, 0)),
        ],
        out_specs=pl.BlockSpec((EXPERT_BLOCK, D_MODEL), lambda i, be, nu: (i, 0)),
        scratch_shapes=[pltpu.VMEM((D_MODEL, D_EXPERT), BF16), pltpu.VMEM((D_MODEL, D_EXPERT), BF16),
                        pltpu.VMEM((D_EXPERT, D_MODEL), BF16)],
    )
    return pl.pallas_call(
        _experts_body,
        out_shape=jax.ShapeDtypeStruct((rows, D_MODEL), BF16),
        grid_spec=grid_spec,
        compiler_params=pltpu.CompilerParams(dimension_semantics=("arbitrary",), vmem_limit_bytes=VMEM_LIMIT),
        name="experts",
    )(block_expert, n_used, xg, wg, wu, wd)


def _combine_body(x_ref, y0_ref, y1_ref, route_ref, g_ref, b_ref, o_ref):
    o_ref[...] = _moe_close(x_ref[...], y0_ref[...], y1_ref[...], route_ref[...], g_ref[...], b_ref[...])


def _combine(x1, y0, y1, route, ln_g, ln_b):
    t = x1.shape[0]
    ts = COMBINE_TILE
    row = lambda c: pl.BlockSpec((ts, c), lambda i: (i, 0))
    return pl.pallas_call(
        _combine_body,
        out_shape=jax.ShapeDtypeStruct((t, D_MODEL), F32),
        grid=(t // ts,),
        in_specs=[row(D_MODEL)] * 3 + [row(128), _const_spec(ln_g.shape), _const_spec(ln_b.shape)],
        out_specs=row(D_MODEL),
        compiler_params=pltpu.CompilerParams(dimension_semantics=("arbitrary",), vmem_limit_bytes=VMEM_LIMIT),
        name="combine",
    )(x1, y0, y1, route, ln_g, ln_b)


def _dispatch(route_t, counts, t):
    expert = route_t[0:2].astype(jnp.int32)
    rank = route_t[4:6].astype(jnp.int32)
    counts = counts[:, 0].astype(jnp.int32)
    padded = (counts + EXPERT_BLOCK - 1) // EXPERT_BLOCK * EXPERT_BLOCK
    pad_end = jnp.cumsum(padded)
    pad_start = pad_end - padded
    dest = rank
    for e in range(N_EXPERTS):
        dest = dest + jnp.where(expert == e, pad_start[e], 0)
    n_blocks = (2 * t + EXPERT_BLOCK - 1) // EXPERT_BLOCK + N_EXPERTS
    block_start = jnp.arange(n_blocks, dtype=jnp.int32) * EXPERT_BLOCK
    block_expert = jnp.minimum(jnp.sum((pad_end[None, :] <= block_start[:, None]).astype(jnp.int32), axis=1),
                               N_EXPERTS - 1)
    n_used = (pad_end[-1:] // EXPERT_BLOCK).astype(jnp.int32)
    by_row = jnp.argsort(dest.reshape(2 * t)).astype(jnp.int32) % t
    pad_before = jnp.cumsum(padded - counts) - (padded - counts)
    src = jnp.arange(n_blocks * EXPERT_BLOCK, dtype=jnp.int32) - jnp.repeat(pad_before[block_expert], EXPERT_BLOCK)
    buf_tok = jnp.take(by_row, src % (2 * t), mode='clip')
    return dest, buf_tok, block_expert, n_used


def _pad_cols(w, width=LORA_PAD):
    return jnp.pad(w, ((0, 0), (0, width - w.shape[1])))


def _pad_rows(w, height=LORA_PAD):
    return jnp.pad(w, ((0, height - w.shape[0]), (0, 0)))


def _row(v):
    return v.reshape(1, -1).astype(F32)


def _layer_params(l, a):
    w_in = a['w_in_first'] if l == 0 else a['w_in_rest'][l - 1]
    if l == 0:
        w_vres = jnp.zeros((D_MODEL, LORA_PAD), F32)
        mu_vres = jnp.zeros((LORA_PAD,), F32)
    else:
        w_vres = _pad_cols(w_in[:, _O_VRES:])
        mu_vres = jnp.pad(a['rwkv_mu_vres'][l - 1], (0, LORA_PAD - 32))
    mu = a['rwkv_mu'][l]
    p = {
        'w_rw': _bf(jnp.concatenate([w_in[:, :_O_WL], _pad_cols(w_in[:, _O_WL:_O_AL]), _pad_cols(w_in[:, _O_AL:_O_GL]),
                                     _pad_cols(w_in[:, _O_GL:_O_SGU]), w_vres], axis=1)),
        'w_sgu': _bf(w_in[:, _O_SGU:_O_CONV]),
        'w_conv': _bf(w_in[:, _O_CONV:_O_GATE]),
        'w_gate': _bf(w_in[:, _O_GATE:_O_VRES]),
        'mu': _row(jnp.concatenate([mu[:_O_WL], jnp.pad(mu[_O_WL:_O_AL], (0, 96)), jnp.pad(mu[_O_AL:_O_GL], (0, 96)),
                                    jnp.pad(mu[_O_GL:], (0, 32)), mu_vres])),
        'w0': _row(a['rwkv_w0'][l]),
        'w_up': _bf(_pad_rows(a['rwkv_w_up'][l])),
        'a0': _row(a['rwkv_a0'][l]),
        'a_up': _bf(_pad_rows(a['rwkv_a_up'][l])),
        'g_up': _bf(_pad_rows(a['rwkv_g_up'][l])),
        'k_k': _row(a['rwkv_k_k'][l]),
        'k_a': _row(a['rwkv_k_a'][l]),
        'r_k': _row(a['rwkv_r_k'][l]),
        'gn_g': _row(a['rwkv_gn_g'][l]),
        'gn_b': _row(a['rwkv_gn_b'][l]),
        'rwkv_w_o': _bf(a['rwkv_w_o'][l]),
        'sgu_ln_g': _row(a['sgu_ln_g'][l]),
        'sgu_ln_b': _row(a['sgu_ln_b'][l]),
        'sgu_w': jnp.transpose(a['sgu_w'][l], (1, 0, 2)).reshape(SGU_CHUNK, SGU_GROUPS * SGU_CHUNK),
        'sgu_bias': jnp.repeat(jnp.transpose(a['sgu_b'][l]), SGU_W // SGU_GROUPS, axis=1),
        'sgu_w_o': _bf(a['sgu_w_o'][l]),
        'conv_dw': jnp.pad(a['conv_dw'][l], ((0, 1), (0, 0))),
        'conv_db': _row(a['conv_db'][l]),
        'conv_ln_g': _row(a['conv_ln_g'][l]),
        'conv_ln_b': _row(a['conv_ln_b'][l]),
        'conv_w_o': _bf(a['conv_w_o'][l]),
        'w_out': _bf(a['w_out'][l]),
        'ln1_g': _row(a['ln1_g'][l]),
        'ln1_b': _row(a['ln1_b'][l]),
        'ln2_g': _row(a['ln2_g'][l]),
        'ln2_b': _row(a['ln2_b'][l]),
    }
    if l > 0:
        p['v0'] = _row(a['rwkv_v0'][l - 1])
        p['v_up'] = _bf(_pad_rows(a['rwkv_v_up'][l - 1]))
    return p


def _forward(a):
    x = a['x']
    bsz, seq, _ = x.shape
    t = bsz * seq
    x = x.reshape(t, D_MODEL)
    head = jnp.arange(HEAD_SLAB, dtype=jnp.int32) // HEAD_DIM
    head_sum = _bf((head[:, None] == head[None, :]).astype(F32))
    rw = _pad_cols(a['router_w'])
    rw_hi = _bf(rw)
    rw_lo = _bf(rw - rw_hi.astype(F32))
    rw_cat = jnp.concatenate([rw_hi, rw_lo], axis=1)
    rb = jnp.broadcast_to(a['router_b'].astype(F32)[:, None], (N_EXPERTS, 128))
    v_first = None
    moe = None
    for l in range(DEPTH):
        p = _layer_params(l, a)
        p.update(head_sum=head_sum, router_hi=rw_hi, router_cat=rw_cat, router_b=rb)
        outs = _mix_in(x, p, v_first, bsz, seq, moe)
        r, lw, k, v, na, nb, g, ga, part = outs[:9]
        if l == 0:
            v_first = v
        else:
            x = outs[9]
        y = _wkv(r, lw, k, v, na, nb, bsz, seq)
        x1, x1b, route, route_t, counts = _mix_out(y, r, k, v, g, ga, part, x, p, bsz, seq)
        dest, buf_tok, block_expert, n_used = _dispatch(route_t, counts, t)
        xg = jnp.take(x1b, buf_tok, axis=0, mode='clip')
        yb = _experts(xg, block_expert, n_used, a['moe_w_gate'], a['moe_w_up'], a['moe_w_down'], l)
        y0 = jnp.take(yb, dest[0], axis=0, mode='clip')
        y1 = jnp.take(yb, dest[1], axis=0, mode='clip')
        moe = (x1, y0, y1, route, p['ln2_g'], p['ln2_b'])
    return _combine(*moe).reshape(bsz, seq, D_MODEL)


def kernel(x, w_in_first, w_in_rest, rwkv_mu, rwkv_mu_vres, rwkv_w0, rwkv_w_up, rwkv_a0, rwkv_a_up, rwkv_g_up, rwkv_v0, rwkv_v_up, rwkv_k_k, rwkv_k_a, rwkv_r_k, rwkv_gn_g, rwkv_gn_b, rwkv_w_o, sgu_ln_g, sgu_ln_b, sgu_w, sgu_b, sgu_w_o, conv_dw, conv_db, conv_ln_g, conv_ln_b, conv_w_o, w_out, ln1_g, ln1_b, router_w, router_b, moe_w_gate, moe_w_up, moe_w_down, ln2_g, ln2_b):
    return _forward(dict(
        x=x, w_in_first=w_in_first, w_in_rest=w_in_rest, rwkv_mu=rwkv_mu, rwkv_mu_vres=rwkv_mu_vres,
        rwkv_w0=rwkv_w0, rwkv_w_up=rwkv_w_up, rwkv_a0=rwkv_a0, rwkv_a_up=rwkv_a_up, rwkv_g_up=rwkv_g_up,
        rwkv_v0=rwkv_v0, rwkv_v_up=rwkv_v_up, rwkv_k_k=rwkv_k_k, rwkv_k_a=rwkv_k_a, rwkv_r_k=rwkv_r_k,
        rwkv_gn_g=rwkv_gn_g, rwkv_gn_b=rwkv_gn_b, rwkv_w_o=rwkv_w_o, sgu_ln_g=sgu_ln_g, sgu_ln_b=sgu_ln_b,
        sgu_w=sgu_w, sgu_b=sgu_b, sgu_w_o=sgu_w_o, conv_dw=conv_dw, conv_db=conv_db, conv_ln_g=conv_ln_g,
        conv_ln_b=conv_ln_b, conv_w_o=conv_w_o, w_out=w_out, ln1_g=ln1_g, ln1_b=ln1_b, router_w=router_w,
        router_b=router_b, moe_w_gate=moe_w_gate, moe_w_up=moe_w_up, moe_w_down=moe_w_down, ln2_g=ln2_g,
        ln2_b=ln2_b))
```
